```python
import math
import jax, jax.numpy as jnp
from jax import lax
import numpy as np

D_MODEL = 4096
BATCH = 4
SEQ = 2048
DEPTH = 1
DEC_BATCH = 32
DEC_SEQ = 1
PAST_LEN = 8192
PAGE_SIZE = 128

FOX_HEADS = D_MODEL // 256
FOX_HEAD_DIM = 128
FOX_WIDTH = FOX_HEADS * FOX_HEAD_DIM
Q_BLOCK = 128
FOX_F_BIAS_INIT = 4.0
M_HEADS = 8
M_QK_DIM = D_MODEL // 16
M_V_DIM = D_MODEL // 8
M_QK_WIDTH = M_HEADS * M_QK_DIM
M_V_WIDTH = M_HEADS * M_V_DIM
M_CHUNK = 64
M_F_BIAS_INIT = 4.0
N_MEM = 256
MEM_HEADS = 4
MEM_HEAD_DIM = D_MODEL // 8
MEM_WIDTH = MEM_HEADS * MEM_HEAD_DIM
N_BRANCH = 3
D_FF = 4 * D_MODEL
EPS = 1e-6

IN_SPLITS = (FOX_WIDTH, FOX_WIDTH, FOX_WIDTH, FOX_HEADS,
             M_QK_WIDTH, M_QK_WIDTH, M_V_WIDTH, M_HEADS, M_HEADS, M_V_WIDTH,
             MEM_WIDTH,
             N_BRANCH * D_MODEL)
D_IN = sum(IN_SPLITS)

kernel_name = 'fox_mlstm_memory_gated_hybrid_step'


def rmsnorm(x, w):
    xf = x.astype(jnp.float32)
    y = xf * lax.rsqrt(jnp.mean(xf * xf, axis=-1, keepdims=True) + EPS)
    return (y * w.astype(jnp.float32)).astype(x.dtype)


def split_points():
    pts, acc = [], 0
    for s in IN_SPLITS[:-1]:
        acc += s
        pts.append(acc)
    return pts


def project_inputs(h, w_in, fox_f_bias, m_i_bias, m_f_bias):
    B, S, _ = h.shape
    parts = jnp.split(h @ w_in, split_points(), axis=-1)
    fq = parts[0].reshape(B, S, FOX_HEADS, FOX_HEAD_DIM)
    fk = parts[1].reshape(B, S, FOX_HEADS, FOX_HEAD_DIM)
    fv = parts[2].reshape(B, S, FOX_HEADS, FOX_HEAD_DIM)
    flf = jax.nn.log_sigmoid((parts[3] + fox_f_bias).astype(jnp.float32))
    mq = parts[4].reshape(B, S, M_HEADS, M_QK_DIM)
    mk = parts[5].reshape(B, S, M_HEADS, M_QK_DIM) * (M_QK_DIM ** -0.5)
    mv = parts[6].reshape(B, S, M_HEADS, M_V_DIM)
    mig = (parts[7] + m_i_bias).astype(jnp.float32)
    mlf = jax.nn.log_sigmoid((parts[8] + m_f_bias).astype(jnp.float32))
    mog = jax.nn.sigmoid(parts[9])
    qq = parts[10].reshape(B, S, MEM_HEADS, MEM_HEAD_DIM)
    gates = jax.nn.sigmoid(parts[11]).reshape(B, S, N_BRANCH, D_MODEL)
    return fq, fk, fv, flf, mq, mk, mv, mig, mlf, mog, qq, gates


def fox_prompt(q, k, v, lf):
    B, S, H, Dh = q.shape
    scale = Dh ** -0.5
    c = jnp.cumsum(lf, axis=1)
    nb = S // Q_BLOCK
    qb = q.reshape(B, nb, Q_BLOCK, H, Dh).swapaxes(0, 1)
    cb = c.reshape(B, nb, Q_BLOCK, H).swapaxes(0, 1)
    c_k = c.transpose(0, 2, 1)
    k_pos = jnp.arange(S)

    def block(args):
        qi, ci, start = args
        s = jnp.einsum('bqhd,bkhd->bhqk', qi, k).astype(jnp.float32) * scale
        s = s + ci.transpose(0, 2, 1)[..., None] - c_k[:, :, None, :]
        q_pos = start + jnp.arange(Q_BLOCK)
        s = jnp.where(k_pos[None, :] <= q_pos[:, None], s, -jnp.inf)
        p = jax.nn.softmax(s, axis=-1)
        return jnp.einsum('bhqk,bkhd->bqhd', p.astype(v.dtype), v)

    o = lax.map(block, (qb, cb, jnp.arange(nb) * Q_BLOCK))
    return o.swapaxes(0, 1).reshape(B, S, H * Dh)


def fox_sample(q, k, v, lf, pool_k, pool_v, pool_lf, page_table, layer):
    DB, DS, H, Dh = q.shape
    P = page_table.shape[1] * PAGE_SIZE
    scale = Dh ** -0.5
    kp = pool_k[layer, page_table].reshape(DB, P, H, Dh)
    vp = pool_v[layer, page_table].reshape(DB, P, H, Dh)
    lfp = pool_lf[layer, page_table].reshape(DB, P, H).astype(jnp.float32)
    c = jnp.cumsum(jnp.concatenate([lfp, lf], axis=1), axis=1).transpose(0, 2, 1)
    c_q = c[:, :, P:]
    s = jnp.concatenate([jnp.einsum('bqhd,bkhd->bhqk', q, kp),
                         jnp.einsum('bqhd,bkhd->bhqk', q, k)], axis=-1).astype(jnp.float32) * scale
    s = s + c_q[..., None] - c[:, :, None, :]
    k_pos = jnp.arange(P + DS)
    q_pos = P + jnp.arange(DS)
    s = jnp.where(k_pos[None, :] <= q_pos[:, None], s, -jnp.inf)
    p = jax.nn.softmax(s, axis=-1)
    o = (jnp.einsum('bhqk,bkhd->bqhd', p[..., :P].astype(vp.dtype), vp)
         + jnp.einsum('bhqk,bkhd->bqhd', p[..., P:].astype(v.dtype), v))
    return o.reshape(DB, DS, H * Dh)


def mlstm_chunkwise(q, k, v, ig, lf, C0, n0, m0, chunk):
    B, S, H, Dk = q.shape
    Dv = v.shape[-1]
    L = min(chunk, S)
    nc = S // L
    causal = jnp.tril(jnp.ones((L, L), dtype=bool))

    def to_chunks(a):
        return a.reshape((B, nc, L) + a.shape[2:]).swapaxes(0, 1)

    def step(carry, xs):
        C, n, m = carry
        qc, kc, vc, igc, lfc = xs
        qf, kf, vf = qc.astype(jnp.float32), kc.astype(jnp.float32), vc.astype(jnp.float32)
        b = jnp.cumsum(lfc, axis=1).transpose(0, 2, 1)
        i_t = igc.transpose(0, 2, 1)
        d = b[..., :, None] - b[..., None, :] + i_t[..., None, :]
        d = jnp.where(causal, d, -jnp.inf)
        inter = b + m[..., None]
        m_t = jnp.maximum(inter, jnp.max(d, axis=-1))
        w_intra = jnp.exp(d - m_t[..., None])
        w_inter = jnp.exp(inter - m_t)
        a = w_intra * jnp.einsum('blhd,bshd->bhls', qf, kf)
        num = (jnp.einsum('bhls,bshe->bhle', a, vf)
               + w_inter[..., None] * jnp.einsum('bhed,blhd->bhle', C, qf))
        den = jnp.sum(a, axis=-1) + w_inter * jnp.einsum('bhd,blhd->bhl', n, qf)
        h = num / jnp.maximum(jnp.abs(den), jnp.exp(-m_t))[..., None]
        m_new = m_t[..., -1]
        g_inter = jnp.exp(b[..., -1] + m - m_new)
        g_intra = jnp.exp(b[..., -1:] - b + i_t - m_new[..., None])
        C_new = g_inter[..., None, None] * C + jnp.einsum('bhs,bshe,bshd->bhed', g_intra, vf, kf)
        n_new = g_inter[..., None] * n + jnp.einsum('bhs,bshd->bhd', g_intra, kf)
        return (C_new, n_new, m_new), h.transpose(0, 2, 1, 3).astype(v.dtype)

    carry0 = (C0.astype(jnp.float32), n0.astype(jnp.float32), m0.astype(jnp.float32))
    xs = (to_chunks(q), to_chunks(k), to_chunks(v), to_chunks(ig), to_chunks(lf))
    (C, n, m), hs = lax.scan(step, carry0, xs)
    h = hs.swapaxes(0, 1).reshape(B, S, H, Dv)
    return h, C, n, m


def mlstm_output(h, og, w):
    B, S, H, Dv = h.shape
    hf = h.astype(jnp.float32)
    hn = hf * lax.rsqrt(jnp.mean(hf * hf, axis=-1, keepdims=True) + EPS)
    hn = hn.reshape(B, S, H * Dv) * w.astype(jnp.float32)
    return (hn * og.astype(jnp.float32)).astype(og.dtype)


def memory_kv(mem, mem_norm_w, w_mem_kv):
    B = mem.shape[0]
    kv = rmsnorm(mem, mem_norm_w) @ w_mem_kv
    mk, mv = jnp.split(kv, 2, axis=-1)
    return (mk.reshape(B, N_MEM, MEM_HEADS, MEM_HEAD_DIM),
            mv.reshape(B, N_MEM, MEM_HEADS, MEM_HEAD_DIM))


def memory_attend(q, mk, mv):
    B, S, H, Dh = q.shape
    s = jnp.einsum('bqhd,bkhd->bhqk', q, mk).astype(jnp.float32) * (Dh ** -0.5)
    p = jax.nn.softmax(s, axis=-1)
    return jnp.einsum('bhqk,bkhd->bqhd', p.astype(mv.dtype), mv).reshape(B, S, H * Dh)


def merge_and_mlp(x, y_fox, y_m, y_mem, gates, w_br_fox, w_br_m, w_br_mem, w_out,
                  ffn_norm_w, w_up, w_down):
    merged = (gates[:, :, 0] * (y_fox @ w_br_fox)
              + gates[:, :, 1] * (y_m @ w_br_m)
              + gates[:, :, 2] * (y_mem @ w_br_mem))
    x = x + merged @ w_out
    h2 = rmsnorm(x, ffn_norm_w)
    return x + jnp.square(jax.nn.relu(h2 @ w_up)) @ w_down


def setup_inputs(seed: int = 0) -> dict:
    key = jax.random.key(seed)
    ks = jax.random.split(key, 32)
    f32 = jnp.float32

    def nrm(k, shape, scale):
        return jax.random.normal(k, shape, f32) * scale

    n_pages = PAST_LEN // PAGE_SIZE
    n_pool = (DEC_BATCH * n_pages * 5) // 4
    page_table = jax.random.permutation(ks[0], n_pool)[:DEC_BATCH * n_pages]
    page_table = page_table.reshape(DEC_BATCH, n_pages).astype(jnp.int32)
    return {
        'x_prompt': nrm(ks[1], (BATCH, SEQ, D_MODEL), 1.0),
        'x_sample': nrm(ks[2], (DEC_BATCH, DEC_SEQ, D_MODEL), 1.0),
        'mem_prompt': nrm(ks[3], (BATCH, N_MEM, D_MODEL), 1.0),
        'cache_fox_k': nrm(ks[4], (DEPTH, n_pool, PAGE_SIZE, FOX_HEADS, FOX_HEAD_DIM), 1.0),
        'cache_fox_v': nrm(ks[5], (DEPTH, n_pool, PAGE_SIZE, FOX_HEADS, FOX_HEAD_DIM), 1.0),
        'cache_fox_logf': jax.nn.log_sigmoid(FOX_F_BIAS_INIT + nrm(ks[6], (DEPTH, n_pool, PAGE_SIZE, FOX_HEADS), 1.0)),
        'state_mlstm_C': nrm(ks[7], (DEPTH, DEC_BATCH, M_HEADS, M_V_DIM, M_QK_DIM), 0.1),
        'state_mlstm_n': nrm(ks[8], (DEPTH, DEC_BATCH, M_HEADS, M_QK_DIM), 0.1),
        'state_mlstm_m': nrm(ks[9], (DEPTH, DEC_BATCH, M_HEADS), 1.0),
        'cache_mem_k': nrm(ks[10], (DEPTH, DEC_BATCH, N_MEM, MEM_HEADS, MEM_HEAD_DIM), 1.0),
        'cache_mem_v': nrm(ks[11], (DEPTH, DEC_BATCH, N_MEM, MEM_HEADS, MEM_HEAD_DIM), 1.0),
        'page_table': page_table,
        'attn_norm_w': 1.0 + nrm(ks[12], (DEPTH, D_MODEL), 0.01),
        'w_in': nrm(ks[13], (DEPTH, D_MODEL, D_IN), D_MODEL ** -0.5),
        'fox_f_bias': FOX_F_BIAS_INIT + nrm(ks[14], (DEPTH, FOX_HEADS), 0.1),
        'm_i_bias': nrm(ks[15], (DEPTH, M_HEADS), 0.1),
        'm_f_bias': M_F_BIAS_INIT + nrm(ks[16], (DEPTH, M_HEADS), 0.1),
        'm_norm_w': 1.0 + nrm(ks[17], (DEPTH, M_V_WIDTH), 0.01),
        'mem_norm_w': 1.0 + nrm(ks[18], (DEPTH, D_MODEL), 0.01),
        'w_mem_kv': nrm(ks[19], (DEPTH, D_MODEL, 2 * MEM_WIDTH), D_MODEL ** -0.5),
        'w_br_fox': nrm(ks[20], (DEPTH, FOX_WIDTH, D_MODEL), FOX_WIDTH ** -0.5),
        'w_br_m': nrm(ks[21], (DEPTH, M_V_WIDTH, D_MODEL), M_V_WIDTH ** -0.5),
        'w_br_mem': nrm(ks[22], (DEPTH, MEM_WIDTH, D_MODEL), MEM_WIDTH ** -0.5),
        'w_out': nrm(ks[23], (DEPTH, D_MODEL, D_MODEL), D_MODEL ** -0.5),
        'ffn_norm_w': 1.0 + nrm(ks[24], (DEPTH, D_MODEL), 0.01),
        'w_up': nrm(ks[25], (DEPTH, D_MODEL, D_FF), D_MODEL ** -0.5),
        'w_down': nrm(ks[26], (DEPTH, D_FF, D_MODEL), D_FF ** -0.5),
        'final_norm_w': 1.0 + nrm(ks[27], (D_MODEL,), 0.01),
    }


def reference(x_prompt, x_sample, mem_prompt, cache_fox_k, cache_fox_v, cache_fox_logf,
              state_mlstm_C, state_mlstm_n, state_mlstm_m, cache_mem_k, cache_mem_v, page_table,
              attn_norm_w, w_in, fox_f_bias, m_i_bias, m_f_bias, m_norm_w, mem_norm_w, w_mem_kv,
              w_br_fox, w_br_m, w_br_mem, w_out, ffn_norm_w, w_up, w_down, final_norm_w):
    xp, xs = x_prompt, x_sample
    bp = xp.shape[0]
    p_fk, p_fv, p_fl, p_C, p_n, p_m, p_mk, p_mv = [], [], [], [], [], [], [], []
    s_fk, s_fv, s_fl, s_C, s_n, s_m = [], [], [], [], [], []
    for l in range(DEPTH):
        (fq, fk, fv, flf, mq, mk, mv, mig, mlf, mog, qq, gates) = project_inputs(
            rmsnorm(xp, attn_norm_w[l]), w_in[l], fox_f_bias[l], m_i_bias[l], m_f_bias[l])
        y_fox = fox_prompt(fq, fk, fv, flf)
        C0 = jnp.zeros((bp, M_HEADS, M_V_DIM, M_QK_DIM), jnp.float32)
        n0 = jnp.zeros((bp, M_HEADS, M_QK_DIM), jnp.float32)
        m0 = jnp.zeros((bp, M_HEADS), jnp.float32)
        hm, C, n, m = mlstm_chunkwise(mq, mk, mv, mig, mlf, C0, n0, m0, M_CHUNK)
        y_m = mlstm_output(hm, mog, m_norm_w[l])
        mem_k, mem_v = memory_kv(mem_prompt, mem_norm_w[l], w_mem_kv[l])
        y_mem = memory_attend(qq, mem_k, mem_v)
        xp = merge_and_mlp(xp, y_fox, y_m, y_mem, gates, w_br_fox[l], w_br_m[l], w_br_mem[l],
                           w_out[l], ffn_norm_w[l], w_up[l], w_down[l])
        p_fk.append(fk); p_fv.append(fv); p_fl.append(flf)
        p_C.append(C); p_n.append(n); p_m.append(m)
        p_mk.append(mem_k); p_mv.append(mem_v)
        (fq, fk, fv, flf, mq, mk, mv, mig, mlf, mog, qq, gates) = project_inputs(
            rmsnorm(xs, attn_norm_w[l]), w_in[l], fox_f_bias[l], m_i_bias[l], m_f_bias[l])
        y_fox = fox_sample(fq, fk, fv, flf, cache_fox_k, cache_fox_v, cache_fox_logf, page_table, l)
        hm, C, n, m = mlstm_chunkwise(mq, mk, mv, mig, mlf, state_mlstm_C[l], state_mlstm_n[l],
                                      state_mlstm_m[l], xs.shape[1])
        y_m = mlstm_output(hm, mog, m_norm_w[l])
        y_mem = memory_attend(qq, cache_mem_k[l], cache_mem_v[l])
        xs = merge_and_mlp(xs, y_fox, y_m, y_mem, gates, w_br_fox[l], w_br_m[l], w_br_mem[l],
                           w_out[l], ffn_norm_w[l], w_up[l], w_down[l])
        s_fk.append(fk); s_fv.append(fv); s_fl.append(flf)
        s_C.append(C); s_n.append(n); s_m.append(m)
    y_prompt = rmsnorm(xp, final_norm_w)
    y_sample = rmsnorm(xs, final_norm_w)
    return (y_prompt, y_sample,
            jnp.stack(p_fk), jnp.stack(p_fv), jnp.stack(p_fl),
            jnp.stack(p_C), jnp.stack(p_n), jnp.stack(p_m),
            jnp.stack(p_mk), jnp.stack(p_mv),
            jnp.stack(s_fk), jnp.stack(s_fv), jnp.stack(s_fl),
            jnp.stack(s_C), jnp.stack(s_n), jnp.stack(s_m))
```

```python
import functools

import jax
import jax.numpy as jnp
from jax import lax
from jax.experimental import pallas as pl
from jax.experimental.pallas import tpu as pltpu

F32 = jnp.float32
BF16 = jnp.bfloat16
HIGHEST = lax.Precision.HIGHEST

EPS = 1e-6
PAGE_SIZE = 128
LANES = 128
VMEM_LIMIT_CAP = 60000 * 1024
VMEM_SLACK = 6 * 1024 * 1024

FOX_LANE0 = 0
MIG_LANE0 = 16
MLF_LANE0 = 24
M_CHUNK = 256
NT_DIMS = (((1,), (1,)), ((), ()))


def _nbytes(shape, dtype):
    n = 1
    for s in shape:
        n *= s
    return n * jnp.dtype(dtype).itemsize


def _params(semantics, block_bytes, extra_bytes=0):
    limit = min(2 * block_bytes + extra_bytes + VMEM_SLACK, VMEM_LIMIT_CAP)
    return pltpu.CompilerParams(dimension_semantics=semantics, vmem_limit_bytes=int(limit))


def _log_sigmoid(x):
    return jnp.minimum(x, 0.0) - jnp.log1p(jnp.exp(-jnp.abs(x)))


def _sigmoid(x):
    return 1.0 / (1.0 + jnp.exp(-x))


def _rmsnorm_body(x_ref, w_ref, o_ref):
    x = x_ref[...].astype(F32)
    y = x * lax.rsqrt(jnp.mean(x * x, axis=-1, keepdims=True) + EPS)
    o_ref[...] = (y * w_ref[...]).astype(o_ref.dtype)


def _rmsnorm(x, w, out_dtype, tm):
    m, d = x.shape
    blocks = _nbytes((tm, d), x.dtype) + _nbytes((tm, d), out_dtype) + _nbytes((1, d), F32)
    return pl.pallas_call(
        _rmsnorm_body,
        grid=(m // tm,),
        in_specs=[pl.BlockSpec((tm, d), lambda i: (i, 0)),
                  pl.BlockSpec((1, d), lambda i: (0, 0))],
        out_specs=pl.BlockSpec((tm, d), lambda i: (i, 0)),
        out_shape=jax.ShapeDtypeStruct((m, d), out_dtype),
        compiler_params=_params(("parallel",), blocks, _nbytes((tm, d), F32) * 2),
        name="rmsnorm",
    )(x, w.reshape(1, d).astype(F32))


def _mm_body(*refs, nk, epilogue, has_resid):
    x_ref, w_ref = refs[0], refs[1]
    r_ref = refs[2] if has_resid else None
    o_ref = refs[2 + has_resid]
    acc_ref = refs[3 + has_resid] if nk > 1 else None
    part = jnp.dot(x_ref[...], w_ref[...], preferred_element_type=F32)

    def finish(acc):
        if epilogue == "sigmoid":
            acc = _sigmoid(acc)
        elif epilogue == "relu2":
            r = jnp.maximum(acc, 0.0)
            acc = r * r
        if has_resid:
            acc = r_ref[...] + acc
        o_ref[...] = acc.astype(o_ref.dtype)

    if nk == 1:
        finish(part)
    else:
        k = pl.program_id(2)

        @pl.when(k == 0)
        def _():
            acc_ref[...] = part

        @pl.when(k > 0)
        def _():
            acc_ref[...] += part

        @pl.when(k == nk - 1)
        def _():
            finish(acc_ref[...])


def _matmul(x, w, *, n_cols, col_off=0, out_dtype, tm, tn, tk=None, epilogue=None, resid=None):
    m, kdim = x.shape
    tk = kdim if tk is None else tk
    nk = kdim // tk
    assert m % tm == 0 and n_cols % tn == 0 and col_off % tn == 0 and kdim % tk == 0
    joff = col_off // tn
    in_specs = [pl.BlockSpec((tm, tk), lambda i, j, k: (i, k)),
                pl.BlockSpec((tk, tn), lambda i, j, k: (k, j + joff))]
    args = [x, w]
    blocks = _nbytes((tm, tk), x.dtype) + _nbytes((tk, tn), w.dtype) + _nbytes((tm, tn), out_dtype)
    if resid is not None:
        in_specs.append(pl.BlockSpec((tm, tn), lambda i, j, k: (i, j)))
        args.append(resid)
        blocks += _nbytes((tm, tn), resid.dtype)
    scratch = [pltpu.VMEM((tm, tn), F32)] if nk > 1 else []
    extra = _nbytes((tm, tn), F32) * (3 if nk > 1 else 2)
    return pl.pallas_call(
        functools.partial(_mm_body, nk=nk, epilogue=epilogue, has_resid=resid is not None),
        grid=(m // tm, n_cols // tn, nk),
        in_specs=in_specs,
        out_specs=pl.BlockSpec((tm, tn), lambda i, j, k: (i, j)),
        out_shape=jax.ShapeDtypeStruct((m, n_cols), out_dtype),
        scratch_shapes=scratch,
        compiler_params=_params(("parallel", "parallel", "arbitrary"), blocks, extra),
        name="matmul_" + (epilogue or "plain"),
    )(*args)


def _merge_body(yf_ref, ym_ref, yq_ref, wf_ref, wm_ref, wq_ref, g0_ref, g1_ref, g2_ref, o_ref):
    acc = g0_ref[...].astype(F32) * jnp.dot(yf_ref[...], wf_ref[...], preferred_element_type=F32)
    acc += g1_ref[...].astype(F32) * jnp.dot(ym_ref[...], wm_ref[...], preferred_element_type=F32)
    acc += g2_ref[...].astype(F32) * jnp.dot(yq_ref[...], wq_ref[...], preferred_element_type=F32)
    o_ref[...] = acc.astype(o_ref.dtype)


def _merge(y_fox, y_m, y_mem, w_fox, w_m, w_mem, gates, gate_off, tm, tn):
    t = y_fox.shape[0]
    d = w_fox.shape[1]
    goff = gate_off // tn
    nd = d // tn
    kf, km, kq = y_fox.shape[1], y_m.shape[1], y_mem.shape[1]
    blocks = (_nbytes((tm, kf + km + kq), BF16) + _nbytes((kf + km + kq, tn), BF16)
              + 4 * _nbytes((tm, tn), BF16))
    return pl.pallas_call(
        _merge_body,
        grid=(t // tm, nd),
        in_specs=[pl.BlockSpec((tm, kf), lambda i, j: (i, 0)),
                  pl.BlockSpec((tm, km), lambda i, j: (i, 0)),
                  pl.BlockSpec((tm, kq), lambda i, j: (i, 0)),
                  pl.BlockSpec((kf, tn), lambda i, j: (0, j)),
                  pl.BlockSpec((km, tn), lambda i, j: (0, j)),
                  pl.BlockSpec((kq, tn), lambda i, j: (0, j)),
                  pl.BlockSpec((tm, tn), lambda i, j: (i, goff + j)),
                  pl.BlockSpec((tm, tn), lambda i, j: (i, goff + nd + j)),
                  pl.BlockSpec((tm, tn), lambda i, j: (i, goff + 2 * nd + j))],
        out_specs=pl.BlockSpec((tm, tn), lambda i, j: (i, j)),
        out_shape=jax.ShapeDtypeStruct((t, d), BF16),
        compiler_params=_params(("parallel", "parallel"), blocks, 4 * _nbytes((tm, tn), F32)),
        name="gated_merge",
    )(y_fox, y_m, y_mem, w_fox, w_m, w_mem, gates, gates, gates)


def _gate_values(x):
    lane = lax.broadcasted_iota(jnp.int32, (1, LANES), 1)
    is_ig = (lane >= MIG_LANE0) & (lane < MLF_LANE0)
    return jnp.where(is_ig, x, _log_sigmoid(x))


def _gate_only_body(sp_ref, bias_ref, g_ref):
    g_ref[...] = _gate_values(sp_ref[...] + bias_ref[...])


def _gates_single(sp, bias):
    t = sp.shape[0]
    spec = pl.BlockSpec((t, LANES), lambda i: (0, 0))
    return pl.pallas_call(
        _gate_only_body,
        grid=(1,),
        in_specs=[spec, pl.BlockSpec((1, LANES), lambda i: (0, 0))],
        out_specs=spec,
        out_shape=jax.ShapeDtypeStruct((t, LANES), F32),
        name="small_gates_single",
    )(sp, bias)


def _gate_body(sp_ref, bias_ref, g_ref, cg_ref, bl_ref, *, seq, chunk):
    rows = chunk
    r = lax.broadcasted_iota(jnp.int32, (rows, rows), 0)
    c = lax.broadcasted_iota(jnp.int32, (rows, rows), 1)
    tri = (r >= c).astype(F32)
    carry = jnp.zeros((1, LANES), F32)
    for t in range(seq // rows):
        sl = slice(t * rows, (t + 1) * rows)
        g = _gate_values(sp_ref[0, sl, :] + bias_ref[...])
        g_ref[0, sl, :] = g
        loc = jnp.dot(tri, g, precision=HIGHEST, preferred_element_type=F32)
        bl_ref[0, sl, :] = loc
        cg_ref[0, sl, :] = loc + carry
        carry = carry + loc[rows - 1:rows, :]


def _gates(sp, bias, chunk):
    b, s, _ = sp.shape
    spec = pl.BlockSpec((1, s, LANES), lambda i: (i, 0, 0))
    shape = jax.ShapeDtypeStruct((b, s, LANES), F32)
    return pl.pallas_call(
        functools.partial(_gate_body, seq=s, chunk=chunk),
        grid=(b,),
        in_specs=[spec, pl.BlockSpec((1, LANES), lambda i: (0, 0))],
        out_specs=[spec, spec, spec],
        out_shape=[shape, shape, shape],
        compiler_params=_params(("parallel",), 4 * _nbytes((s, LANES), F32), _nbytes((s, LANES), F32)),
        name="small_gates",
    )(sp, bias)


def _lane_column(tile, lane_index):
    lane = lax.broadcasted_iota(jnp.int32, (1, LANES), 1)
    return jnp.sum(jnp.where(lane == lane_index, tile, 0.0), axis=1, keepdims=True)


def _fox_prompt_body(q_ref, k_ref, v_ref, cq_ref, ck_ref, o_ref, m_sc, l_sc, acc_sc, *, scale):
    h = pl.program_id(1)
    qi = pl.program_id(2)
    ki = pl.program_id(3)
    tq, tk = q_ref.shape[0], k_ref.shape[0]

    @pl.when(ki == 0)
    def _():
        m_sc[...] = jnp.full(m_sc.shape, -jnp.inf, F32)
        l_sc[...] = jnp.zeros(l_sc.shape, F32)
        acc_sc[...] = jnp.zeros(acc_sc.shape, F32)

    def step(diagonal):
        s = lax.dot_general(q_ref[...], k_ref[...].astype(BF16), NT_DIMS,
                            preferred_element_type=F32) * scale
        s = s + _lane_column(cq_ref[0], h) - ck_ref[0, 0]
        if diagonal:
            r = lax.broadcasted_iota(jnp.int32, (tq, tk), 0)
            c = lax.broadcasted_iota(jnp.int32, (tq, tk), 1)
            s = jnp.where(c <= r, s, -jnp.inf)
        m_prev = m_sc[...]
        m_new = jnp.maximum(m_prev, jnp.max(s, axis=1, keepdims=True))
        alpha = jnp.exp(m_prev - m_new)
        p = jnp.exp(s - m_new)
        l_sc[...] = alpha * l_sc[...] + jnp.sum(p, axis=1, keepdims=True)
        acc_sc[...] = alpha * acc_sc[...] + jnp.dot(
            p.astype(BF16), v_ref[...].astype(BF16), preferred_element_type=F32)
        m_sc[...] = m_new

    @pl.when(ki < qi)
    def _():
        step(False)

    @pl.when(ki == qi)
    def _():
        step(True)
        o_ref[...] = (acc_sc[...] / l_sc[...]).astype(o_ref.dtype)


def _fox_prompt(qsrc, q_col0, k, v, cq, ck, batch, seq, heads, dh, tq):
    nq = seq // tq
    qoff = q_col0 // dh
    blocks = (_nbytes((tq, dh), BF16) * 2 + 2 * _nbytes((tq, dh), F32)
              + _nbytes((tq, LANES), F32) + _nbytes((8, tq), F32))
    return pl.pallas_call(
        functools.partial(_fox_prompt_body, scale=dh ** -0.5),
        grid=(batch, heads, nq, nq),
        in_specs=[pl.BlockSpec((tq, dh), lambda b, h, i, j: (b * nq + i, qoff + h)),
                  pl.BlockSpec((tq, dh), lambda b, h, i, j: (b * nq + jnp.minimum(i, j), h)),
                  pl.BlockSpec((tq, dh), lambda b, h, i, j: (b * nq + jnp.minimum(i, j), h)),
                  pl.BlockSpec((1, tq, LANES), lambda b, h, i, j: (b, i, 0)),
                  pl.BlockSpec((1, 1, 1, tq), lambda b, h, i, j: (b, h, 0, jnp.minimum(i, j)))],
        out_specs=pl.BlockSpec((tq, dh), lambda b, h, i, j: (b * nq + i, h)),
        out_shape=jax.ShapeDtypeStruct((batch * seq, heads * dh), BF16),
        scratch_shapes=[pltpu.VMEM((tq, 1), F32), pltpu.VMEM((tq, 1), F32), pltpu.VMEM((tq, dh), F32)],
        compiler_params=_params(("parallel", "parallel", "parallel", "arbitrary"), blocks,
                                6 * _nbytes((tq, tq), F32)),
        name="fox_prompt_attention",
    )(qsrc, k, v, cq, ck)


def _mem_prompt_body(q_ref, k_ref, v_ref, o_ref, *, scale):
    s = lax.dot_general(q_ref[...], k_ref[...].astype(BF16), NT_DIMS,
                        preferred_element_type=F32) * scale
    m = jnp.max(s, axis=1, keepdims=True)
    p = jnp.exp(s - m)
    p = p / jnp.sum(p, axis=1, keepdims=True)
    o_ref[...] = jnp.dot(p.astype(BF16), v_ref[...].astype(BF16),
                         preferred_element_type=F32).astype(o_ref.dtype)


def _mem_prompt(qsrc, q_col0, k, v, batch, seq, n_mem, heads, dh, tq):
    nq = seq // tq
    qoff = q_col0 // dh
    blocks = 2 * _nbytes((tq, dh), BF16) + 2 * _nbytes((n_mem, dh), F32)
    return pl.pallas_call(
        functools.partial(_mem_prompt_body, scale=dh ** -0.5),
        grid=(batch, heads, nq),
        in_specs=[pl.BlockSpec((tq, dh), lambda b, h, i: (b * nq + i, qoff + h)),
                  pl.BlockSpec((n_mem, dh), lambda b, h, i: (b, h)),
                  pl.BlockSpec((n_mem, dh), lambda b, h, i: (b, h))],
        out_specs=pl.BlockSpec((tq, dh), lambda b, h, i: (b * nq + i, h)),
        out_shape=jax.ShapeDtypeStruct((batch * seq, heads * dh), BF16),
        compiler_params=_params(("parallel", "parallel", "parallel"), blocks,
                                4 * _nbytes((tq, n_mem), F32) + _nbytes((tq, dh), F32)),
        name="memory_prompt_attention",
    )(qsrc, k, v)


HEAD_ROWS = 16


def _head_masks(heads, dh):
    assert heads <= HEAD_ROWS
    row = lax.broadcasted_iota(jnp.int32, (HEAD_ROWS, heads * dh), 0)
    col = lax.broadcasted_iota(jnp.int32, (HEAD_ROWS, heads * dh), 1)
    return (col >= row * dh) & (col < (row + 1) * dh)


def _block_diag(row, mask):
    return jnp.where(mask, jnp.broadcast_to(row.astype(F32), mask.shape), 0.0)


def _decode_block(k_blk, v_blk, bias, qbd, scale, m_sc, l_sc, acc_sc):
    s = lax.dot_general(qbd, k_blk.astype(BF16), NT_DIMS, preferred_element_type=F32) * scale
    if bias is not None:
        s = s + bias
    m_prev = m_sc[...]
    m_new = jnp.maximum(m_prev, jnp.max(s, axis=1, keepdims=True))
    alpha = jnp.exp(m_prev - m_new)
    p = jnp.exp(s - m_new)
    l_sc[...] = alpha * l_sc[...] + jnp.sum(p, axis=1, keepdims=True)
    acc_sc[...] = alpha * acc_sc[...] + jnp.dot(p.astype(BF16), v_blk.astype(BF16),
                                               preferred_element_type=F32)
    m_sc[...] = m_new


def _decode_output(mask, l_sc, acc_sc):
    return jnp.sum(jnp.where(mask, acc_sc[...] / l_sc[...], 0.0), axis=0, keepdims=True)


def _split_bf16(x):
    hi = x.astype(BF16)
    r1 = x - hi.astype(F32)
    mid = r1.astype(BF16)
    lo = (r1 - mid.astype(F32)).astype(BF16)
    return hi, mid, lo


def _fox_decode_body(pt_ref, q_ref, kn_ref, vn_ref, lfn_ref, *refs, heads, dh, pages_per_step, scale):
    npg = pages_per_step
    k_refs = refs[0:npg]
    v_refs = refs[npg:2 * npg]
    lf_refs = refs[2 * npg:3 * npg]
    o_ref = refs[3 * npg]
    spread_sc, m_sc, l_sc, acc_sc, carry_sc = refs[3 * npg + 1:]
    j = pl.program_id(1)
    flat = PAGE_SIZE * heads

    @pl.when(j == 0)
    def _():
        m_sc[...] = jnp.sum(q_ref[0].astype(F32) * kn_ref[0], axis=1, keepdims=True) * scale
        l_sc[...] = jnp.ones(l_sc.shape, F32)
        acc_sc[...] = vn_ref[0]
        carry_sc[...] = lfn_ref[0]
        key_of = lax.broadcasted_iota(jnp.int32, (PAGE_SIZE, flat), 1) // heads
        spread_sc[...] = (key_of == lax.broadcasted_iota(jnp.int32, (PAGE_SIZE, flat), 0)).astype(BF16)

    r = lax.broadcasted_iota(jnp.int32, (PAGE_SIZE, PAGE_SIZE), 0)
    c = lax.broadcasted_iota(jnp.int32, (PAGE_SIZE, PAGE_SIZE), 1)
    later = (r > c).astype(F32)
    carry = carry_sc[...]
    pieces = []
    for t in range(npg):
        lf = lf_refs[t][0]
        pieces.extend(_split_bf16(carry + jnp.dot(lf, later, precision=HIGHEST,
                                                  preferred_element_type=F32)))
        carry = carry + jnp.sum(lf, axis=1, keepdims=True)
    carry_sc[...] = carry
    bias_flat = jnp.dot(jnp.concatenate(pieces, axis=0), spread_sc[...], preferred_element_type=F32)

    own_head = (lax.broadcasted_iota(jnp.int32, (heads, flat), 1) % heads
                == lax.broadcasted_iota(jnp.int32, (heads, flat), 0))
    q = q_ref[0]
    for t in range(npg):
        b0 = 3 * heads * t
        bias = (bias_flat[b0:b0 + heads] + bias_flat[b0 + heads:b0 + 2 * heads]
                + bias_flat[b0 + 2 * heads:b0 + 3 * heads])
        k2 = k_refs[t][0].reshape(flat, dh).astype(BF16)
        v2 = v_refs[t][0].reshape(flat, dh).astype(BF16)
        s = lax.dot_general(q, k2, NT_DIMS, preferred_element_type=F32) * scale + bias
        s = jnp.where(own_head, s, -jnp.inf)
        m_prev = m_sc[...]
        m_new = jnp.maximum(m_prev, jnp.max(s, axis=1, keepdims=True))
        alpha = jnp.exp(m_prev - m_new)
        p = jnp.exp(s - m_new)
        l_sc[...] = alpha * l_sc[...] + jnp.sum(p, axis=1, keepdims=True)
        acc_sc[...] = alpha * acc_sc[...] + jnp.dot(p.astype(BF16), v2, preferred_element_type=F32)
        m_sc[...] = m_new

    @pl.when(j == pl.num_programs(1) - 1)
    def _():
        o_ref[0] = (acc_sc[...] / l_sc[...]).astype(o_ref.dtype)


def _fox_decode(q, k_new, v_new, lf_new, pool_k, pool_v, pool_lf_t, page_table, pages_per_step):
    db, heads, dh = q.shape
    assert heads == HEAD_ROWS and pool_k.shape[1] == PAGE_SIZE
    n_pages = page_table.shape[1]
    npg = pages_per_step
    steps = n_pages // npg

    def page_map(t, rank):
        return lambda b, j, pt: (pt[b, n_pages - 1 - (j * npg + t)],) + (0,) * (rank - 1)

    row = lambda b, j, pt: (b, 0, 0)
    in_specs = [pl.BlockSpec((1, heads, dh), row), pl.BlockSpec((1, heads, dh), row),
                pl.BlockSpec((1, heads, dh), row), pl.BlockSpec((1, heads, 1), row)]
    in_specs += [pl.BlockSpec((1, PAGE_SIZE, heads, dh), page_map(t, 4)) for t in range(npg)]
    in_specs += [pl.BlockSpec((1, PAGE_SIZE, heads, dh), page_map(t, 4)) for t in range(npg)]
    in_specs += [pl.BlockSpec((1, heads, PAGE_SIZE), page_map(t, 3)) for t in range(npg)]
    page_bytes = _nbytes((PAGE_SIZE, heads, dh), F32)
    blocks = npg * (2 * page_bytes + _nbytes((heads, PAGE_SIZE), F32))
    return pl.pallas_call(
        functools.partial(_fox_decode_body, heads=heads, dh=dh, pages_per_step=npg, scale=dh ** -0.5),
        grid_spec=pltpu.PrefetchScalarGridSpec(
            num_scalar_prefetch=1,
            grid=(db, steps),
            in_specs=in_specs,
            out_specs=pl.BlockSpec((1, heads, dh), row),
            scratch_shapes=[pltpu.VMEM((PAGE_SIZE, PAGE_SIZE * heads), BF16),
                            pltpu.VMEM((heads, 1), F32), pltpu.VMEM((heads, 1), F32),
                            pltpu.VMEM((heads, dh), F32), pltpu.VMEM((heads, 1), F32)]),
        out_shape=jax.ShapeDtypeStruct((db, heads, dh), BF16),
        compiler_params=_params(("parallel", "arbitrary"), blocks, 4 * page_bytes),
        name="fox_decode_attention",
    )(page_table, q, k_new, v_new, lf_new, *([pool_k] * npg), *([pool_v] * npg), *([pool_lf_t] * npg))


def _mem_decode_body(q_ref, k_ref, v_ref, o_ref, m_sc, l_sc, acc_sc, *, heads, dh, scale):
    mask = _head_masks(heads, dh)
    m_sc[...] = jnp.full(m_sc.shape, -jnp.inf, F32)
    l_sc[...] = jnp.zeros(l_sc.shape, F32)
    acc_sc[...] = jnp.zeros(acc_sc.shape, F32)
    qbd = _block_diag(q_ref[0], mask).astype(BF16)
    _decode_block(k_ref[0], v_ref[0], None, qbd, scale, m_sc, l_sc, acc_sc)
    o_ref[0] = _decode_output(mask, l_sc, acc_sc).astype(o_ref.dtype)


def _mem_decode(q, k, v, heads, dh):
    db, n_mem, width = k.shape
    blocks = 2 * _nbytes((n_mem, width), F32) + 2 * _nbytes((1, width), F32)
    return pl.pallas_call(
        functools.partial(_mem_decode_body, heads=heads, dh=dh, scale=dh ** -0.5),
        grid=(db,),
        in_specs=[pl.BlockSpec((1, 1, width), lambda b: (b, 0, 0)),
                  pl.BlockSpec((1, n_mem, width), lambda b: (b, 0, 0)),
                  pl.BlockSpec((1, n_mem, width), lambda b: (b, 0, 0))],
        out_specs=pl.BlockSpec((1, 1, width), lambda b: (b, 0, 0)),
        out_shape=jax.ShapeDtypeStruct((db, 1, width), BF16),
        scratch_shapes=[pltpu.VMEM((HEAD_ROWS, 1), F32), pltpu.VMEM((HEAD_ROWS, 1), F32),
                        pltpu.VMEM((HEAD_ROWS, width), F32)],
        compiler_params=_params(("parallel",), blocks, 4 * _nbytes((n_mem, width), F32)),
        name="memory_decode_attention",
    )(q.reshape(db, 1, width), k, v)


def _head_norm_gate(h, w_row, og):
    hn = h * lax.rsqrt(jnp.mean(h * h, axis=-1, keepdims=True) + EPS)
    return hn * w_row * og.astype(F32)


def _mlstm_chunk_body(q_ref, k_ref, v_ref, og_ref, w_ref, gcol_ref, bcol_ref, irow_ref, brow_ref,
                      y_ref, c_ref, n_ref, m_ref, *, k_scale):
    h = pl.program_id(1)
    ci = pl.program_id(2)
    L = q_ref.shape[0]

    @pl.when(ci == 0)
    def _():
        c_ref[...] = jnp.zeros(c_ref.shape, F32)
        n_ref[...] = jnp.zeros(n_ref.shape, F32)
        m_ref[...] = jnp.zeros(m_ref.shape, F32)

    q = q_ref[...]
    kf = k_ref[...].astype(F32) * k_scale
    k = kf.astype(BF16)
    v = v_ref[...]
    c_state = c_ref[0, 0]
    n_state = n_ref[0, 0]
    m_prev = m_ref[0, 0][:, 0:1]
    b_col = _lane_column(bcol_ref[0], MLF_LANE0 + h)
    i_col = _lane_column(gcol_ref[0], MIG_LANE0 + h)
    b_row = brow_ref[0, 0]
    i_row = irow_ref[0, 0]

    r = lax.broadcasted_iota(jnp.int32, (L, L), 0)
    c = lax.broadcasted_iota(jnp.int32, (L, L), 1)
    d = jnp.where(c <= r, b_col - b_row + i_row, -jnp.inf)
    inter = b_col + m_prev
    m_t = jnp.maximum(inter, jnp.max(d, axis=1, keepdims=True))
    w_intra = jnp.exp(d - m_t)
    w_inter = jnp.exp(inter - m_t)
    a = w_intra * lax.dot_general(q, k, NT_DIMS, preferred_element_type=F32)
    num = (jnp.dot(a.astype(BF16), v, preferred_element_type=F32)
           + w_inter * lax.dot_general(q, c_state.astype(BF16), NT_DIMS, preferred_element_type=F32))
    den = (jnp.sum(a, axis=1, keepdims=True)
           + w_inter * jnp.sum(q.astype(F32) * n_state, axis=1, keepdims=True))
    hcur = num / jnp.maximum(jnp.abs(den), jnp.exp(-m_t))
    y_ref[...] = _head_norm_gate(hcur, w_ref[...], og_ref[...]).astype(y_ref.dtype)

    m_new = m_t[L - 1:L, :]
    b_last = b_col[L - 1:L, :]
    g_inter = jnp.exp(b_last + m_prev - m_new)
    g_intra = jnp.exp(b_last - b_col + i_col - m_new)
    kg = g_intra * kf
    c_ref[0, 0] = g_inter * c_state + lax.dot_general(
        v, kg.astype(BF16), (((0,), (0,)), ((), ())), preferred_element_type=F32)
    n_ref[0, 0] = g_inter * n_state + jnp.sum(kg, axis=0, keepdims=True)
    m_ref[0, 0] = jnp.broadcast_to(m_new, (1, LANES))


def _mlstm_prompt(p1, q_col0, k_col0, v_col0, p2, og_col0, norm_w, g, b_loc, i_row, b_row,
                  batch, seq, heads, dk, dv, k_scale):
    L = M_CHUNK
    nc = seq // L
    qo, ko, vo, oo = q_col0 // dk, k_col0 // dk, v_col0 // dv, og_col0 // dv
    blocks = (2 * _nbytes((L, dk), BF16) + 3 * _nbytes((L, dv), BF16) + 2 * _nbytes((L, LANES), F32)
              + _nbytes((dv, dk), F32))
    return pl.pallas_call(
        functools.partial(_mlstm_chunk_body, k_scale=k_scale),
        grid=(batch, heads, nc),
        in_specs=[pl.BlockSpec((L, dk), lambda b, h, c: (b * nc + c, qo + h)),
                  pl.BlockSpec((L, dk), lambda b, h, c: (b * nc + c, ko + h)),
                  pl.BlockSpec((L, dv), lambda b, h, c: (b * nc + c, vo + h)),
                  pl.BlockSpec((L, dv), lambda b, h, c: (b * nc + c, oo + h)),
                  pl.BlockSpec((1, dv), lambda b, h, c: (0, h)),
                  pl.BlockSpec((1, L, LANES), lambda b, h, c: (b, c, 0)),
                  pl.BlockSpec((1, L, LANES), lambda b, h, c: (b, c, 0)),
                  pl.BlockSpec((1, 1, 1, L), lambda b, h, c: (b, h, 0, c)),
                  pl.BlockSpec((1, 1, 1, L), lambda b, h, c: (b, h, 0, c))],
        out_specs=[pl.BlockSpec((L, dv), lambda b, h, c: (b * nc + c, h)),
                   pl.BlockSpec((1, 1, dv, dk), lambda b, h, c: (b, h, 0, 0)),
                   pl.BlockSpec((1, 1, 1, dk), lambda b, h, c: (b, h, 0, 0)),
                   pl.BlockSpec((1, 1, 1, LANES), lambda b, h, c: (b, h, 0, 0))],
        out_shape=[jax.ShapeDtypeStruct((batch * seq, heads * dv), BF16),
                   jax.ShapeDtypeStruct((batch, heads, dv, dk), F32),
                   jax.ShapeDtypeStruct((batch, heads, 1, dk), F32),
                   jax.ShapeDtypeStruct((batch, heads, 1, LANES), F32)],
        compiler_params=_params(("parallel", "parallel", "arbitrary"), blocks,
                                8 * _nbytes((L, L), F32) + 4 * _nbytes((L, dv), F32)
                                + 3 * _nbytes((dv, dk), F32)),
        name="mlstm_chunkwise",
    )(p1, p1, p1, p2, norm_w, g, b_loc, i_row, b_row)


def _mlstm_step_body(q_ref, k_ref, v_ref, og_ref, w_ref, g_ref, c_ref, n_ref, m_ref,
                     y_ref, co_ref, no_ref, mo_ref, *, heads, dk, dv, k_scale):
    g = g_ref[0]
    eye = (lax.broadcasted_iota(jnp.int32, (dv, dv), 0)
           == lax.broadcasted_iota(jnp.int32, (dv, dv), 1)).astype(BF16)
    lane = lax.broadcasted_iota(jnp.int32, (1, LANES), 1)
    m_out = jnp.zeros((1, LANES), F32)
    for h in range(heads):
        q = q_ref[0, :, h * dk:(h + 1) * dk].astype(F32)
        kf = k_ref[0, :, h * dk:(h + 1) * dk].astype(F32) * k_scale
        v = v_ref[0, :, h * dv:(h + 1) * dv].astype(F32)
        ig = g[:, MIG_LANE0 + h:MIG_LANE0 + h + 1]
        lf = g[:, MLF_LANE0 + h:MLF_LANE0 + h + 1]
        m_prev = m_ref[0][:, h:h + 1]
        c_state = c_ref[0, h]
        n_state = n_ref[0, h]
        inter = lf + m_prev
        m_t = jnp.maximum(inter, ig)
        w_intra = jnp.exp(ig - m_t)
        w_inter = jnp.exp(inter - m_t)
        a = w_intra * jnp.sum(q * kf, axis=1, keepdims=True)
        q_rows = jnp.broadcast_to(q, (HEAD_ROWS, dk)).astype(BF16)
        cq = lax.dot_general(q_rows, c_state.astype(BF16), NT_DIMS, preferred_element_type=F32)[0:1]
        num = a * v + w_inter * cq
        den = a + w_inter * jnp.sum(q * n_state, axis=1, keepdims=True)
        hcur = num / jnp.maximum(jnp.abs(den), jnp.exp(-m_t))
        y_ref[0, :, h * dv:(h + 1) * dv] = _head_norm_gate(
            hcur, w_ref[:, h * dv:(h + 1) * dv], og_ref[0, :, h * dv:(h + 1) * dv]).astype(y_ref.dtype)
        v_rows = jnp.broadcast_to(v, (HEAD_ROWS, dv)).astype(BF16)
        v_col = lax.dot_general(eye, v_rows, NT_DIMS, preferred_element_type=F32)[:, 0:1]
        kg = w_intra * kf
        co_ref[0, h] = w_inter * c_state + v_col * kg
        no_ref[0, h] = w_inter * n_state + kg
        m_out = jnp.where(lane == h, m_t, m_out)
    mo_ref[0] = m_out


def _mlstm_step(q, k, v, og, norm_w, g, c0, n0, m0, heads, dk, dv, k_scale):
    db = q.shape[0]
    row = lambda b: (b, 0, 0)
    st = lambda b: (b, 0, 0, 0)
    blocks = 2 * _nbytes((heads, dv, dk), F32) + 4 * _nbytes((1, heads * dv), F32)
    return pl.pallas_call(
        functools.partial(_mlstm_step_body, heads=heads, dk=dk, dv=dv, k_scale=k_scale),
        grid=(db,),
        in_specs=[pl.BlockSpec((1, 1, heads * dk), row), pl.BlockSpec((1, 1, heads * dk), row),
                  pl.BlockSpec((1, 1, heads * dv), row), pl.BlockSpec((1, 1, heads * dv), row),
                  pl.BlockSpec((1, heads * dv), lambda b: (0, 0)),
                  pl.BlockSpec((1, 1, LANES), row),
                  pl.BlockSpec((1, heads, dv, dk), st), pl.BlockSpec((1, heads, 1, dk), st),
                  pl.BlockSpec((1, 1, heads), row)],
        out_specs=[pl.BlockSpec((1, 1, heads * dv), row),
                   pl.BlockSpec((1, heads, dv, dk), st), pl.BlockSpec((1, heads, 1, dk), st),
                   pl.BlockSpec((1, 1, LANES), row)],
        out_shape=[jax.ShapeDtypeStruct((db, 1, heads * dv), BF16),
                   jax.ShapeDtypeStruct((db, heads, dv, dk), F32),
                   jax.ShapeDtypeStruct((db, heads, 1, dk), F32),
                   jax.ShapeDtypeStruct((db, 1, LANES), F32)],
        compiler_params=_params(("parallel",), blocks, 6 * _nbytes((dv, dk), F32)),
        name="mlstm_step",
    )(q, k, v, og, norm_w, g, c0, n0, m0)


def _bf16_cols(w, *ranges):
    return jnp.concatenate([w[:, a:b].astype(BF16) for a, b in ranges], axis=1)


def kernel(x_prompt, x_sample, mem_prompt, cache_fox_k, cache_fox_v, cache_fox_logf, state_mlstm_C, state_mlstm_n, state_mlstm_m, cache_mem_k, cache_mem_v, page_table, attn_norm_w, w_in, fox_f_bias, m_i_bias, m_f_bias, m_norm_w, mem_norm_w, w_mem_kv, w_br_fox, w_br_m, w_br_mem, w_out, ffn_norm_w, w_up, w_down, final_norm_w):
    batch, seq, d_model = x_prompt.shape
    dec_batch, dec_seq, _ = x_sample.shape
    depth = w_in.shape[0]
    assert depth == 1 and dec_seq == 1
    fox_heads, fox_dh = cache_fox_k.shape[3], cache_fox_k.shape[4]
    m_heads, m_dv, m_dk = state_mlstm_C.shape[2:]
    n_mem, mem_heads, mem_dh = cache_mem_k.shape[2:]
    fox_w, mqk_w, mv_w, mem_w = fox_heads * fox_dh, m_heads * m_dk, m_heads * m_dv, mem_heads * mem_dh
    assert fox_heads == MIG_LANE0 and m_heads == MLF_LANE0 - MIG_LANE0
    tp, ts = batch * seq, dec_batch * dec_seq
    k_scale = m_dk ** -0.5

    sizes = (fox_w, fox_w, fox_w, fox_heads, mqk_w, mqk_w, mv_w, m_heads, m_heads, mv_w, mem_w,
             3 * d_model)
    offs = [0]
    for s in sizes:
        offs.append(offs[-1] + s)
    seg = lambda i: (offs[i], offs[i + 1])

    l = 0
    w = w_in[l]
    w_p1 = _bf16_cols(w, seg(0), seg(4), seg(5), seg(6), seg(10))
    w_kv = _bf16_cols(w, (offs[1], offs[3]))
    w_p2 = _bf16_cols(w, seg(9), seg(11))
    n_small = fox_heads + 2 * m_heads
    w_sm = jnp.concatenate([_bf16_cols(w, seg(3), seg(7), seg(8)),
                            jnp.zeros((d_model, LANES - n_small), BF16)], axis=1)
    bias_sm = jnp.concatenate([fox_f_bias[l], m_i_bias[l], m_f_bias[l],
                               jnp.zeros((LANES - n_small,), F32)]).reshape(1, LANES).astype(F32)
    P1_FQ, P1_MQ, P1_MK, P1_MV, P1_QQ = 0, fox_w, fox_w + mqk_w, fox_w + 2 * mqk_w, fox_w + 2 * mqk_w + mv_w
    n_p1 = P1_QQ + mem_w
    P2_OG, P2_GATES = 0, mv_w
    n_p2 = mv_w + 3 * d_model
    w_memkv_b = w_mem_kv[l].astype(BF16)
    w_brf, w_brm, w_brq = w_br_fox[l].astype(BF16), w_br_m[l].astype(BF16), w_br_mem[l].astype(BF16)
    w_out_b, w_up_b, w_down_b = w_out[l].astype(BF16), w_up[l].astype(BF16), w_down[l].astype(BF16)
    d_ff = w_up_b.shape[1]
    m_norm = m_norm_w[l].reshape(1, mv_w).astype(F32)

    def project(x2d, tm):
        h = _rmsnorm(x2d, attn_norm_w[l], BF16, min(tm, 256))
        tn = 512
        p1 = _matmul(h, w_p1, n_cols=n_p1, out_dtype=BF16, tm=tm, tn=tn)
        fk = _matmul(h, w_kv, n_cols=fox_w, col_off=0, out_dtype=F32, tm=tm, tn=tn)
        fv = _matmul(h, w_kv, n_cols=fox_w, col_off=fox_w, out_dtype=F32, tm=tm, tn=tn)
        p2 = _matmul(h, w_p2, n_cols=n_p2, out_dtype=BF16, tm=tm, tn=tn, epilogue="sigmoid")
        sp = _matmul(h, w_sm, n_cols=LANES, out_dtype=F32, tm=tm, tn=LANES)
        return p1, fk, fv, p2, sp

    def mix_and_mlp(x2d, y_fox, y_m, y_mem, p2, tm):
        merged = _merge(y_fox, y_m, y_mem, w_brf, w_brm, w_brq, p2, P2_GATES, min(tm, 512), 512)
        x1 = _matmul(merged, w_out_b, n_cols=d_model, out_dtype=F32, tm=tm, tn=512, resid=x2d)
        h2 = _rmsnorm(x1, ffn_norm_w[l], BF16, min(tm, 256))
        up = _matmul(h2, w_up_b, n_cols=d_ff, out_dtype=BF16, tm=tm, tn=512, epilogue="relu2")
        x2 = _matmul(up, w_down_b, n_cols=d_model, out_dtype=F32, tm=tm, tn=1024, tk=2048, resid=x1)
        return _rmsnorm(x2, final_norm_w, F32, min(tm, 256))

    xp = x_prompt.reshape(tp, d_model)
    p1, fk, fv, p2, sp = project(xp, 1024)
    g, c_glob, b_loc = _gates(sp.reshape(batch, seq, LANES), bias_sm, M_CHUNK)
    ck = c_glob[:, :, FOX_LANE0:FOX_LANE0 + fox_heads].transpose(0, 2, 1).reshape(batch, fox_heads, 1, seq)
    i_row = g[:, :, MIG_LANE0:MIG_LANE0 + m_heads].transpose(0, 2, 1).reshape(batch, m_heads, 1, seq)
    b_row = b_loc[:, :, MLF_LANE0:MLF_LANE0 + m_heads].transpose(0, 2, 1).reshape(batch, m_heads, 1, seq)

    y_fox = _fox_prompt(p1, P1_FQ, fk, fv, c_glob, ck, batch, seq, fox_heads, fox_dh, 512)
    y_m, p_c, p_n, p_m = _mlstm_prompt(p1, P1_MQ, P1_MK, P1_MV, p2, P2_OG, m_norm, g, b_loc, i_row, b_row,
                                       batch, seq, m_heads, m_dk, m_dv, k_scale)
    mem_h = _rmsnorm(mem_prompt.reshape(batch * n_mem, d_model), mem_norm_w[l], BF16, 256)
    mem_k = _matmul(mem_h, w_memkv_b, n_cols=mem_w, col_off=0, out_dtype=F32, tm=batch * n_mem, tn=512)
    mem_v = _matmul(mem_h, w_memkv_b, n_cols=mem_w, col_off=mem_w, out_dtype=F32, tm=batch * n_mem, tn=512)
    y_mem = _mem_prompt(p1, P1_QQ, mem_k, mem_v, batch, seq, n_mem, mem_heads, mem_dh, 512)
    y_prompt = mix_and_mlp(xp, y_fox, y_m, y_mem, p2, 1024).reshape(batch, seq, d_model)

    xs = x_sample.reshape(ts, d_model)
    s1, sfk, sfv, s2, ssp = project(xs, ts)
    sg = _gates_single(ssp, bias_sm)
    s_flf = sg[:, FOX_LANE0:FOX_LANE0 + fox_heads]
    per_head = lambda a: a.reshape(ts, fox_heads, fox_dh)
    sy_fox = _fox_decode(
        per_head(s1[:, P1_FQ:P1_FQ + fox_w]), per_head(sfk), per_head(sfv), s_flf.reshape(ts, fox_heads, 1),
        cache_fox_k[l], cache_fox_v[l], cache_fox_logf[l].transpose(0, 2, 1), page_table, 4).reshape(ts, fox_w)
    row3 = lambda a: a.reshape(ts, 1, a.shape[-1])
    sy_m, s_c, s_n, s_m = _mlstm_step(
        row3(s1[:, P1_MQ:P1_MQ + mqk_w]), row3(s1[:, P1_MK:P1_MK + mqk_w]), row3(s1[:, P1_MV:P1_MV + mv_w]),
        row3(s2[:, P2_OG:P2_OG + mv_w]), m_norm, row3(sg),
        state_mlstm_C[l], state_mlstm_n[l].reshape(ts, m_heads, 1, m_dk),
        state_mlstm_m[l].reshape(ts, 1, m_heads), m_heads, m_dk, m_dv, k_scale)
    sy_mem = _mem_decode(s1[:, P1_QQ:P1_QQ + mem_w], cache_mem_k[l].reshape(ts, n_mem, mem_w),
                         cache_mem_v[l].reshape(ts, n_mem, mem_w), mem_heads, mem_dh).reshape(ts, mem_w)
    y_sample = mix_and_mlp(xs, sy_fox, sy_m.reshape(ts, mv_w), sy_mem, s2, ts).reshape(dec_batch, dec_seq, d_model)

    lead = lambda a, shape: a.reshape((1,) + shape)
    return (
        y_prompt, y_sample,
        lead(fk, (batch, seq, fox_heads, fox_dh)), lead(fv, (batch, seq, fox_heads, fox_dh)),
        lead(g[:, :, FOX_LANE0:FOX_LANE0 + fox_heads], (batch, seq, fox_heads)),
        lead(p_c, (batch, m_heads, m_dv, m_dk)), lead(p_n, (batch, m_heads, m_dk)),
        lead(p_m[:, :, 0, 0], (batch, m_heads)),
        lead(mem_k, (batch, n_mem, mem_heads, mem_dh)), lead(mem_v, (batch, n_mem, mem_heads, mem_dh)),
        lead(sfk, (dec_batch, dec_seq, fox_heads, fox_dh)), lead(sfv, (dec_batch, dec_seq, fox_heads, fox_dh)),
        lead(s_flf, (dec_batch, dec_seq, fox_heads)),
        lead(s_c, (dec_batch, m_heads, m_dv, m_dk)), lead(s_n, (dec_batch, m_heads, m_dk)),
        lead(s_m[:, 0, :m_heads], (dec_batch, m_heads)),
    )
```

```python
import functools

import jax
import jax.numpy as jnp
from jax import lax
from jax.experimental import pallas as pl
from jax.experimental.pallas import tpu as pltpu

F32 = jnp.float32
BF16 = jnp.bfloat16
HIGHEST = lax.Precision.HIGHEST

EPS = 1e-6
PAGE_SIZE = 128
LANES = 128
VMEM_LIMIT_CAP = 60000 * 1024
VMEM_SLACK = 6 * 1024 * 1024

FOX_LANE0 = 0
MIG_LANE0 = 16
MLF_LANE0 = 24
M_CHUNK = 256
NT_DIMS = (((1,), (1,)), ((), ()))


def _nbytes(shape, dtype):
    n = 1
    for s in shape:
        n *= s
    return n * jnp.dtype(dtype).itemsize


def _params(semantics, block_bytes, extra_bytes=0):
    limit = min(2 * block_bytes + extra_bytes + VMEM_SLACK, VMEM_LIMIT_CAP)
    return pltpu.CompilerParams(dimension_semantics=semantics, vmem_limit_bytes=int(limit))


def _log_sigmoid(x):
    return jnp.minimum(x, 0.0) - jnp.log1p(jnp.exp(-jnp.abs(x)))


def _sigmoid(x):
    return 1.0 / (1.0 + jnp.exp(-x))


def _rmsnorm_body(x_ref, w_ref, o_ref):
    x = x_ref[...].astype(F32)
    y = x * lax.rsqrt(jnp.mean(x * x, axis=-1, keepdims=True) + EPS)
    o_ref[...] = (y * w_ref[...]).astype(o_ref.dtype)


def _rmsnorm(x, w, out_dtype, tm):
    m, d = x.shape
    blocks = _nbytes((tm, d), x.dtype) + _nbytes((tm, d), out_dtype) + _nbytes((1, d), F32)
    return pl.pallas_call(
        _rmsnorm_body,
        grid=(m // tm,),
        in_specs=[pl.BlockSpec((tm, d), lambda i: (i, 0)),
                  pl.BlockSpec((1, d), lambda i: (0, 0))],
        out_specs=pl.BlockSpec((tm, d), lambda i: (i, 0)),
        out_shape=jax.ShapeDtypeStruct((m, d), out_dtype),
        compiler_params=_params(("parallel",), blocks, _nbytes((tm, d), F32) * 2),
        name="rmsnorm",
    )(x, w.reshape(1, d).astype(F32))


def _shift_cast_body(*refs, shift):
    a_ref, o_ref = refs[0], refs[-1]
    a = a_ref[...]
    if shift:
        wide = jnp.concatenate([a, refs[1][...]], axis=1)
        a = pltpu.roll(wide, wide.shape[1] - shift, axis=1)[:, :a.shape[1]]
    o_ref[...] = a.astype(o_ref.dtype)


def _shift_cast(w, col0, n_cols, tr, tn):
    rows = w.shape[0]
    base = col0 // LANES * LANES
    shift = col0 - base
    assert rows % tr == 0 and n_cols % tn == 0 and base % tn == 0 and tn % LANES == 0
    assert col0 + n_cols <= w.shape[1]
    jb = base // tn
    in_specs = [pl.BlockSpec((tr, tn), lambda i, j: (i, jb + j))]
    args = [w]
    if shift:
        in_specs.append(pl.BlockSpec((tr, LANES), lambda i, j: (i, (jb + j + 1) * (tn // LANES))))
        args.append(w)
    blocks = _nbytes((tr, tn + LANES), F32) + _nbytes((tr, tn), BF16)
    return pl.pallas_call(
        functools.partial(_shift_cast_body, shift=shift),
        grid=(rows // tr, n_cols // tn),
        in_specs=in_specs,
        out_specs=pl.BlockSpec((tr, tn), lambda i, j: (i, j)),
        out_shape=jax.ShapeDtypeStruct((rows, n_cols), BF16),
        compiler_params=_params(("parallel", "parallel"), blocks, 3 * _nbytes((tr, tn + LANES), F32)),
        name="shift_cast",
    )(*args)


def _mm_body(*refs, nk, epilogue, plain_blocks, has_resid):
    x_ref, w_ref = refs[0], refs[1]
    r_ref = refs[2] if has_resid else None
    o_ref = refs[2 + has_resid]
    acc_ref = refs[3 + has_resid] if nk > 1 else None
    j = pl.program_id(1)
    part = jnp.dot(x_ref[...], w_ref[...], preferred_element_type=F32)

    def finish(acc):
        if epilogue == "sigmoid":
            gated = _sigmoid(acc)
            if plain_blocks is not None:
                gated = jnp.where((j >= plain_blocks[0]) & (j < plain_blocks[1]), acc, gated)
            acc = gated
        elif epilogue == "relu2":
            r = jnp.maximum(acc, 0.0)
            acc = r * r
        if has_resid:
            acc = r_ref[...] + acc
        o_ref[...] = acc.astype(o_ref.dtype)

    if nk == 1:
        finish(part)
    else:
        k = pl.program_id(2)

        @pl.when(k == 0)
        def _():
            acc_ref[...] = part

        @pl.when(k > 0)
        def _():
            acc_ref[...] += part

        @pl.when(k == nk - 1)
        def _():
            finish(acc_ref[...])


def _matmul(x, w, *, n_cols, col_off=0, out_dtype, tm, tn, tk=None, epilogue=None, plain_cols=None,
            resid=None):
    m, kdim = x.shape
    tk = kdim if tk is None else tk
    nk = kdim // tk
    assert m % tm == 0 and n_cols % tn == 0 and col_off % tn == 0 and kdim % tk == 0
    joff = col_off // tn
    plain_blocks = None
    if plain_cols is not None:
        assert epilogue == "sigmoid" and plain_cols[0] % tn == 0 and plain_cols[1] % tn == 0
        plain_blocks = (plain_cols[0] // tn, plain_cols[1] // tn)
    in_specs = [pl.BlockSpec((tm, tk), lambda i, j, k: (i, k)),
                pl.BlockSpec((tk, tn), lambda i, j, k: (k, j + joff))]
    args = [x, w]
    blocks = _nbytes((tm, tk), x.dtype) + _nbytes((tk, tn), w.dtype) + _nbytes((tm, tn), out_dtype)
    if resid is not None:
        in_specs.append(pl.BlockSpec((tm, tn), lambda i, j, k: (i, j)))
        args.append(resid)
        blocks += _nbytes((tm, tn), resid.dtype)
    scratch = [pltpu.VMEM((tm, tn), F32)] if nk > 1 else []
    extra = _nbytes((tm, tn), F32) * (3 if nk > 1 else 2)
    return pl.pallas_call(
        functools.partial(_mm_body, nk=nk, epilogue=epilogue, plain_blocks=plain_blocks,
                          has_resid=resid is not None),
        grid=(m // tm, n_cols // tn, nk),
        in_specs=in_specs,
        out_specs=pl.BlockSpec((tm, tn), lambda i, j, k: (i, j)),
        out_shape=jax.ShapeDtypeStruct((m, n_cols), out_dtype),
        scratch_shapes=scratch,
        compiler_params=_params(("parallel", "parallel", "arbitrary"), blocks, extra),
        name="matmul_" + (epilogue or "plain"),
    )(*args)


def _merge_body(yf_ref, ym_ref, yq_ref, wf_ref, wm_ref, wq_ref, g0_ref, g1_ref, g2_ref, o_ref):
    acc = g0_ref[...].astype(F32) * jnp.dot(yf_ref[...], wf_ref[...], preferred_element_type=F32)
    acc += g1_ref[...].astype(F32) * jnp.dot(ym_ref[...], wm_ref[...], preferred_element_type=F32)
    acc += g2_ref[...].astype(F32) * jnp.dot(yq_ref[...], wq_ref[...], preferred_element_type=F32)
    o_ref[...] = acc.astype(o_ref.dtype)


def _merge(y_fox, y_m, y_mem, w_fox, w_m, w_mem, gates, gate_off, tm, tn):
    t = y_fox.shape[0]
    d = w_fox.shape[1]
    goff = gate_off // tn
    nd = d // tn
    kf, km, kq = y_fox.shape[1], y_m.shape[1], y_mem.shape[1]
    blocks = (_nbytes((tm, kf + km + kq), BF16) + _nbytes((kf + km + kq, tn), BF16)
              + 4 * _nbytes((tm, tn), BF16))
    return pl.pallas_call(
        _merge_body,
        grid=(t // tm, nd),
        in_specs=[pl.BlockSpec((tm, kf), lambda i, j: (i, 0)),
                  pl.BlockSpec((tm, km), lambda i, j: (i, 0)),
                  pl.BlockSpec((tm, kq), lambda i, j: (i, 0)),
                  pl.BlockSpec((kf, tn), lambda i, j: (0, j)),
                  pl.BlockSpec((km, tn), lambda i, j: (0, j)),
                  pl.BlockSpec((kq, tn), lambda i, j: (0, j)),
                  pl.BlockSpec((tm, tn), lambda i, j: (i, goff + j)),
                  pl.BlockSpec((tm, tn), lambda i, j: (i, goff + nd + j)),
                  pl.BlockSpec((tm, tn), lambda i, j: (i, goff + 2 * nd + j))],
        out_specs=pl.BlockSpec((tm, tn), lambda i, j: (i, j)),
        out_shape=jax.ShapeDtypeStruct((t, d), BF16),
        compiler_params=_params(("parallel", "parallel"), blocks, 4 * _nbytes((tm, tn), F32)),
        name="gated_merge",
    )(y_fox, y_m, y_mem, w_fox, w_m, w_mem, gates, gates, gates)


def _gate_values(x):
    lane = lax.broadcasted_iota(jnp.int32, (1, LANES), 1)
    is_ig = (lane >= MIG_LANE0) & (lane < MLF_LANE0)
    return jnp.where(is_ig, x, _log_sigmoid(x))


def _gate_only_body(sp_ref, bias_ref, g_ref):
    g_ref[...] = _gate_values(sp_ref[...] + bias_ref[...])


def _gates_single(sp, bias):
    t = sp.shape[0]
    spec = pl.BlockSpec((t, LANES), lambda i: (0, 0))
    return pl.pallas_call(
        _gate_only_body,
        grid=(1,),
        in_specs=[spec, pl.BlockSpec((1, LANES), lambda i: (0, 0))],
        out_specs=spec,
        out_shape=jax.ShapeDtypeStruct((t, LANES), F32),
        name="small_gates_single",
    )(sp, bias)


def _gate_body(sp_ref, bias_ref, g_ref, cg_ref, bl_ref, *, seq, chunk):
    rows = chunk
    r = lax.broadcasted_iota(jnp.int32, (rows, rows), 0)
    c = lax.broadcasted_iota(jnp.int32, (rows, rows), 1)
    tri = (r >= c).astype(F32)
    carry = jnp.zeros((1, LANES), F32)
    for t in range(seq // rows):
        sl = slice(t * rows, (t + 1) * rows)
        g = _gate_values(sp_ref[0, sl, :] + bias_ref[...])
        g_ref[0, sl, :] = g
        loc = jnp.dot(tri, g, precision=HIGHEST, preferred_element_type=F32)
        bl_ref[0, sl, :] = loc
        cg_ref[0, sl, :] = loc + carry
        carry = carry + loc[rows - 1:rows, :]


def _gates(sp, bias, chunk):
    b, s, _ = sp.shape
    spec = pl.BlockSpec((1, s, LANES), lambda i: (i, 0, 0))
    shape = jax.ShapeDtypeStruct((b, s, LANES), F32)
    return pl.pallas_call(
        functools.partial(_gate_body, seq=s, chunk=chunk),
        grid=(b,),
        in_specs=[spec, pl.BlockSpec((1, LANES), lambda i: (0, 0))],
        out_specs=[spec, spec, spec],
        out_shape=[shape, shape, shape],
        compiler_params=_params(("parallel",), 4 * _nbytes((s, LANES), F32), _nbytes((s, LANES), F32)),
        name="small_gates",
    )(sp, bias)


def _lane_column(tile, lane_index):
    lane = lax.broadcasted_iota(jnp.int32, (1, LANES), 1)
    return jnp.sum(jnp.where(lane == lane_index, tile, 0.0), axis=1, keepdims=True)


def _fox_prompt_body(q_ref, k_ref, v_ref, cq_ref, ck_ref, o_ref, kb_sc, vb_sc, *, scale, tq):
    h = pl.program_id(1)
    seq = q_ref.shape[0]
    kb_sc[...] = k_ref[...].astype(BF16)
    vb_sc[...] = v_ref[...].astype(BF16)
    r = lax.broadcasted_iota(jnp.int32, (tq, tq), 0)
    c = lax.broadcasted_iota(jnp.int32, (tq, tq), 1)
    causal = c <= r
    for i in range(seq // tq):
        lo, hi = i * tq, (i + 1) * tq
        q = q_ref[lo:hi, :]
        cq = _lane_column(cq_ref[0, lo:hi, :], h)

        def scores(a, b):
            s = lax.dot_general(q, kb_sc[a:b, :], NT_DIMS, preferred_element_type=F32) * scale
            return s + cq - ck_ref[0, 0, :, a:b]

        s_d = jnp.where(causal, scores(lo, hi), -jnp.inf)
        m = jnp.max(s_d, axis=1, keepdims=True)
        if i > 0:
            s_o = scores(0, lo)
            m = jnp.maximum(m, jnp.max(s_o, axis=1, keepdims=True))
        p_d = jnp.exp(s_d - m)
        l = jnp.sum(p_d, axis=1, keepdims=True)
        acc = jnp.dot(p_d.astype(BF16), vb_sc[lo:hi, :], preferred_element_type=F32)
        if i > 0:
            p_o = jnp.exp(s_o - m)
            l = l + jnp.sum(p_o, axis=1, keepdims=True)
            acc = acc + jnp.dot(p_o.astype(BF16), vb_sc[0:lo, :], preferred_element_type=F32)
        o_ref[lo:hi, :] = (acc / l).astype(o_ref.dtype)


def _fox_prompt(qsrc, q_col0, k, v, cq, ck, batch, seq, heads, dh, tq):
    qoff = q_col0 // dh
    blocks = (2 * _nbytes((seq, dh), BF16) + 2 * _nbytes((seq, dh), F32)
              + _nbytes((seq, LANES), F32) + _nbytes((8, seq), F32))
    return pl.pallas_call(
        functools.partial(_fox_prompt_body, scale=dh ** -0.5, tq=tq),
        grid=(batch, heads),
        in_specs=[pl.BlockSpec((seq, dh), lambda b, h: (b, qoff + h)),
                  pl.BlockSpec((seq, dh), lambda b, h: (b, h)),
                  pl.BlockSpec((seq, dh), lambda b, h: (b, h)),
                  pl.BlockSpec((1, seq, LANES), lambda b, h: (b, 0, 0)),
                  pl.BlockSpec((1, 1, 1, seq), lambda b, h: (b, h, 0, 0))],
        out_specs=pl.BlockSpec((seq, dh), lambda b, h: (b, h)),
        out_shape=jax.ShapeDtypeStruct((batch * seq, heads * dh), BF16),
        scratch_shapes=[pltpu.VMEM((seq, dh), BF16), pltpu.VMEM((seq, dh), BF16)],
        compiler_params=_params(("parallel", "parallel"), blocks, 8 * _nbytes((tq, seq), F32)),
        name="fox_prompt_attention",
    )(qsrc, k, v, cq, ck)


def _mem_prompt_body(q_ref, k_ref, v_ref, o_ref, *, scale):
    s = lax.dot_general(q_ref[...], k_ref[...].astype(BF16), NT_DIMS,
                        preferred_element_type=F32) * scale
    m = jnp.max(s, axis=1, keepdims=True)
    p = jnp.exp(s - m)
    p = p / jnp.sum(p, axis=1, keepdims=True)
    o_ref[...] = jnp.dot(p.astype(BF16), v_ref[...].astype(BF16),
                         preferred_element_type=F32).astype(o_ref.dtype)


def _mem_prompt(qsrc, q_col0, k, v, batch, seq, n_mem, heads, dh, tq):
    nq = seq // tq
    qoff = q_col0 // dh
    blocks = 2 * _nbytes((tq, dh), BF16) + 2 * _nbytes((n_mem, dh), F32)
    return pl.pallas_call(
        functools.partial(_mem_prompt_body, scale=dh ** -0.5),
        grid=(batch, heads, nq),
        in_specs=[pl.BlockSpec((tq, dh), lambda b, h, i: (b * nq + i, qoff + h)),
                  pl.BlockSpec((n_mem, dh), lambda b, h, i: (b, h)),
                  pl.BlockSpec((n_mem, dh), lambda b, h, i: (b, h))],
        out_specs=pl.BlockSpec((tq, dh), lambda b, h, i: (b * nq + i, h)),
        out_shape=jax.ShapeDtypeStruct((batch * seq, heads * dh), BF16),
        compiler_params=_params(("parallel", "parallel", "parallel"), blocks,
                                4 * _nbytes((tq, n_mem), F32) + _nbytes((tq, dh), F32)),
        name="memory_prompt_attention",
    )(qsrc, k, v)


HEAD_ROWS = 16


def _head_masks(heads, dh):
    assert heads <= HEAD_ROWS
    row = lax.broadcasted_iota(jnp.int32, (HEAD_ROWS, heads * dh), 0)
    col = lax.broadcasted_iota(jnp.int32, (HEAD_ROWS, heads * dh), 1)
    return (col >= row * dh) & (col < (row + 1) * dh)


def _block_diag(row, mask):
    return jnp.where(mask, jnp.broadcast_to(row.astype(F32), mask.shape), 0.0)


def _decode_block(k_blk, v_blk, bias, qbd, scale, m_sc, l_sc, acc_sc):
    s = lax.dot_general(qbd, k_blk.astype(BF16), NT_DIMS, preferred_element_type=F32) * scale
    if bias is not None:
        s = s + bias
    m_prev = m_sc[...]
    m_new = jnp.maximum(m_prev, jnp.max(s, axis=1, keepdims=True))
    alpha = jnp.exp(m_prev - m_new)
    p = jnp.exp(s - m_new)
    l_sc[...] = alpha * l_sc[...] + jnp.sum(p, axis=1, keepdims=True)
    acc_sc[...] = alpha * acc_sc[...] + jnp.dot(p.astype(BF16), v_blk.astype(BF16),
                                               preferred_element_type=F32)
    m_sc[...] = m_new


def _decode_output(mask, l_sc, acc_sc):
    return jnp.sum(jnp.where(mask, acc_sc[...] / l_sc[...], 0.0), axis=0, keepdims=True)


def _split_bf16(x):
    hi = x.astype(BF16)
    r1 = x - hi.astype(F32)
    mid = r1.astype(BF16)
    lo = (r1 - mid.astype(F32)).astype(BF16)
    return hi, mid, lo


def _fox_decode_body(pt_ref, q_ref, kn_ref, vn_ref, lfn_ref, *refs, heads, dh, pages_per_step, scale):
    npg = pages_per_step
    k_refs = refs[0:npg]
    v_refs = refs[npg:2 * npg]
    lf_refs = refs[2 * npg:3 * npg]
    o_ref = refs[3 * npg]
    spread_sc, m_sc, l_sc, acc_sc, carry_sc = refs[3 * npg + 1:]
    j = pl.program_id(1)
    flat = PAGE_SIZE * heads

    @pl.when(j == 0)
    def _():
        m_sc[...] = jnp.sum(q_ref[0].astype(F32) * kn_ref[0], axis=1, keepdims=True) * scale
        l_sc[...] = jnp.ones(l_sc.shape, F32)
        acc_sc[...] = vn_ref[0]
        carry_sc[...] = lfn_ref[0]
        key_of = lax.broadcasted_iota(jnp.int32, (PAGE_SIZE, flat), 1) // heads
        spread_sc[...] = (key_of == lax.broadcasted_iota(jnp.int32, (PAGE_SIZE, flat), 0)).astype(BF16)

    r = lax.broadcasted_iota(jnp.int32, (PAGE_SIZE, PAGE_SIZE), 0)
    c = lax.broadcasted_iota(jnp.int32, (PAGE_SIZE, PAGE_SIZE), 1)
    later = (r > c).astype(F32)
    carry = carry_sc[...]
    pieces = []
    for t in range(npg):
        lf = lf_refs[t][0]
        pieces.extend(_split_bf16(carry + jnp.dot(lf, later, precision=HIGHEST,
                                                  preferred_element_type=F32)))
        carry = carry + jnp.sum(lf, axis=1, keepdims=True)
    carry_sc[...] = carry
    bias_flat = jnp.dot(jnp.concatenate(pieces, axis=0), spread_sc[...], preferred_element_type=F32)

    own_head = (lax.broadcasted_iota(jnp.int32, (heads, flat), 1) % heads
                == lax.broadcasted_iota(jnp.int32, (heads, flat), 0))
    q = q_ref[0]
    scores = []
    for t in range(npg):
        b0 = 3 * heads * t
        bias = (bias_flat[b0:b0 + heads] + bias_flat[b0 + heads:b0 + 2 * heads]
                + bias_flat[b0 + 2 * heads:b0 + 3 * heads])
        k2 = k_refs[t][0].reshape(flat, dh).astype(BF16)
        s = lax.dot_general(q, k2, NT_DIMS, preferred_element_type=F32) * scale + bias
        scores.append(jnp.where(own_head, s, -jnp.inf))
    m_prev = m_sc[...]
    m_new = m_prev
    for s in scores:
        m_new = jnp.maximum(m_new, jnp.max(s, axis=1, keepdims=True))
    alpha = jnp.exp(m_prev - m_new)
    l_new = alpha * l_sc[...]
    acc = alpha * acc_sc[...]
    for t in range(npg):
        p = jnp.exp(scores[t] - m_new)
        l_new = l_new + jnp.sum(p, axis=1, keepdims=True)
        v2 = v_refs[t][0].reshape(flat, dh).astype(BF16)
        acc = acc + jnp.dot(p.astype(BF16), v2, preferred_element_type=F32)
    l_sc[...] = l_new
    acc_sc[...] = acc
    m_sc[...] = m_new

    @pl.when(j == pl.num_programs(1) - 1)
    def _():
        o_ref[0] = (acc_sc[...] / l_sc[...]).astype(o_ref.dtype)


def _fox_decode(q, k_new, v_new, lf_new, pool_k, pool_v, pool_lf_t, page_table, pages_per_step):
    db, heads, dh = q.shape
    assert heads == HEAD_ROWS and pool_k.shape[1] == PAGE_SIZE
    n_pages = page_table.shape[1]
    npg = pages_per_step
    steps = n_pages // npg

    def page_map(t, rank):
        return lambda b, j, pt: (pt[b, n_pages - 1 - (j * npg + t)],) + (0,) * (rank - 1)

    row = lambda b, j, pt: (b, 0, 0)
    in_specs = [pl.BlockSpec((1, heads, dh), row), pl.BlockSpec((1, heads, dh), row),
                pl.BlockSpec((1, heads, dh), row), pl.BlockSpec((1, heads, 1), row)]
    in_specs += [pl.BlockSpec((1, PAGE_SIZE, heads, dh), page_map(t, 4)) for t in range(npg)]
    in_specs += [pl.BlockSpec((1, PAGE_SIZE, heads, dh), page_map(t, 4)) for t in range(npg)]
    in_specs += [pl.BlockSpec((1, heads, PAGE_SIZE), page_map(t, 3)) for t in range(npg)]
    page_bytes = _nbytes((PAGE_SIZE, heads, dh), F32)
    blocks = npg * (2 * page_bytes + _nbytes((heads, PAGE_SIZE), F32))
    return pl.pallas_call(
        functools.partial(_fox_decode_body, heads=heads, dh=dh, pages_per_step=npg, scale=dh ** -0.5),
        grid_spec=pltpu.PrefetchScalarGridSpec(
            num_scalar_prefetch=1,
            grid=(db, steps),
            in_specs=in_specs,
            out_specs=pl.BlockSpec((1, heads, dh), row),
            scratch_shapes=[pltpu.VMEM((PAGE_SIZE, PAGE_SIZE * heads), BF16),
                            pltpu.VMEM((heads, 1), F32), pltpu.VMEM((heads, 1), F32),
                            pltpu.VMEM((heads, dh), F32), pltpu.VMEM((heads, 1), F32)]),
        out_shape=jax.ShapeDtypeStruct((db, heads, dh), BF16),
        compiler_params=_params(("parallel", "arbitrary"), blocks, 4 * page_bytes),
        name="fox_decode_attention",
    )(page_table, q, k_new, v_new, lf_new, *([pool_k] * npg), *([pool_v] * npg), *([pool_lf_t] * npg))


def _mem_decode_body(q_ref, k_ref, v_ref, o_ref, m_sc, l_sc, acc_sc, *, heads, dh, scale):
    mask = _head_masks(heads, dh)
    m_sc[...] = jnp.full(m_sc.shape, -jnp.inf, F32)
    l_sc[...] = jnp.zeros(l_sc.shape, F32)
    acc_sc[...] = jnp.zeros(acc_sc.shape, F32)
    qbd = _block_diag(q_ref[0], mask).astype(BF16)
    _decode_block(k_ref[0], v_ref[0], None, qbd, scale, m_sc, l_sc, acc_sc)
    o_ref[0] = _decode_output(mask, l_sc, acc_sc).astype(o_ref.dtype)


def _mem_decode(q, k, v, heads, dh):
    db, n_mem, width = k.shape
    blocks = 2 * _nbytes((n_mem, width), F32) + 2 * _nbytes((1, width), F32)
    return pl.pallas_call(
        functools.partial(_mem_decode_body, heads=heads, dh=dh, scale=dh ** -0.5),
        grid=(db,),
        in_specs=[pl.BlockSpec((1, 1, width), lambda b: (b, 0, 0)),
                  pl.BlockSpec((1, n_mem, width), lambda b: (b, 0, 0)),
                  pl.BlockSpec((1, n_mem, width), lambda b: (b, 0, 0))],
        out_specs=pl.BlockSpec((1, 1, width), lambda b: (b, 0, 0)),
        out_shape=jax.ShapeDtypeStruct((db, 1, width), BF16),
        scratch_shapes=[pltpu.VMEM((HEAD_ROWS, 1), F32), pltpu.VMEM((HEAD_ROWS, 1), F32),
                        pltpu.VMEM((HEAD_ROWS, width), F32)],
        compiler_params=_params(("parallel",), blocks, 4 * _nbytes((n_mem, width), F32)),
        name="memory_decode_attention",
    )(q.reshape(db, 1, width), k, v)


def _head_norm_gate(h, w_row, og):
    hn = h * lax.rsqrt(jnp.mean(h * h, axis=-1, keepdims=True) + EPS)
    return hn * w_row * og.astype(F32)


def _mlstm_chunk_body(q_ref, k_ref, v_ref, og_ref, w_ref, gcol_ref, bcol_ref, irow_ref, brow_ref,
                      y_ref, c_ref, n_ref, m_ref, *, k_scale):
    h = pl.program_id(1)
    ci = pl.program_id(2)
    L = q_ref.shape[0]

    @pl.when(ci == 0)
    def _():
        c_ref[...] = jnp.zeros(c_ref.shape, F32)
        n_ref[...] = jnp.zeros(n_ref.shape, F32)
        m_ref[...] = jnp.zeros(m_ref.shape, F32)

    q = q_ref[...]
    kf = k_ref[...].astype(F32) * k_scale
    k = kf.astype(BF16)
    v = v_ref[...]
    c_state = c_ref[0, 0]
    n_state = n_ref[0, 0]
    m_prev = m_ref[0, 0][:, 0:1]
    b_col = _lane_column(bcol_ref[0], MLF_LANE0 + h)
    i_col = _lane_column(gcol_ref[0], MIG_LANE0 + h)
    b_row = brow_ref[0, 0]
    i_row = irow_ref[0, 0]

    r = lax.broadcasted_iota(jnp.int32, (L, L), 0)
    c = lax.broadcasted_iota(jnp.int32, (L, L), 1)
    d = jnp.where(c <= r, b_col - b_row + i_row, -jnp.inf)
    inter = b_col + m_prev
    m_t = jnp.maximum(inter, jnp.max(d, axis=1, keepdims=True))
    w_intra = jnp.exp(d - m_t)
    w_inter = jnp.exp(inter - m_t)
    a = w_intra * lax.dot_general(q, k, NT_DIMS, preferred_element_type=F32)
    num = (jnp.dot(a.astype(BF16), v, preferred_element_type=F32)
           + w_inter * lax.dot_general(q, c_state.astype(BF16), NT_DIMS, preferred_element_type=F32))
    den = (jnp.sum(a, axis=1, keepdims=True)
           + w_inter * jnp.sum(q.astype(F32) * n_state, axis=1, keepdims=True))
    hcur = num / jnp.maximum(jnp.abs(den), jnp.exp(-m_t))
    y_ref[...] = _head_norm_gate(hcur, w_ref[...], og_ref[...]).astype(y_ref.dtype)

    m_new = m_t[L - 1:L, :]
    b_last = b_col[L - 1:L, :]
    g_inter = jnp.exp(b_last + m_prev - m_new)
    g_intra = jnp.exp(b_last - b_col + i_col - m_new)
    kg = g_intra * kf
    c_ref[0, 0] = g_inter * c_state + lax.dot_general(
        v, kg.astype(BF16), (((0,), (0,)), ((), ())), preferred_element_type=F32)
    n_ref[0, 0] = g_inter * n_state + jnp.sum(kg, axis=0, keepdims=True)
    m_ref[0, 0] = jnp.broadcast_to(m_new, (1, LANES))


def _mlstm_prompt(p1, q_col0, k_col0, v_col0, p2, og_col0, norm_w, g, b_loc, i_row, b_row,
                  batch, seq, heads, dk, dv, k_scale):
    L = M_CHUNK
    nc = seq // L
    qo, ko, vo, oo = q_col0 // dk, k_col0 // dk, v_col0 // dv, og_col0 // dv
    blocks = (2 * _nbytes((L, dk), BF16) + 3 * _nbytes((L, dv), BF16) + 2 * _nbytes((L, LANES), F32)
              + _nbytes((dv, dk), F32))
    return pl.pallas_call(
        functools.partial(_mlstm_chunk_body, k_scale=k_scale),
        grid=(batch, heads, nc),
        in_specs=[pl.BlockSpec((L, dk), lambda b, h, c: (b * nc + c, qo + h)),
                  pl.BlockSpec((L, dk), lambda b, h, c: (b * nc + c, ko + h)),
                  pl.BlockSpec((L, dv), lambda b, h, c: (b * nc + c, vo + h)),
                  pl.BlockSpec((L, dv), lambda b, h, c: (b * nc + c, oo + h)),
                  pl.BlockSpec((1, dv), lambda b, h, c: (0, h)),
                  pl.BlockSpec((1, L, LANES), lambda b, h, c: (b, c, 0)),
                  pl.BlockSpec((1, L, LANES), lambda b, h, c: (b, c, 0)),
                  pl.BlockSpec((1, 1, 1, L), lambda b, h, c: (b, h, 0, c)),
                  pl.BlockSpec((1, 1, 1, L), lambda b, h, c: (b, h, 0, c))],
        out_specs=[pl.BlockSpec((L, dv), lambda b, h, c: (b * nc + c, h)),
                   pl.BlockSpec((1, 1, dv, dk), lambda b, h, c: (b, h, 0, 0)),
                   pl.BlockSpec((1, 1, 1, dk), lambda b, h, c: (b, h, 0, 0)),
                   pl.BlockSpec((1, 1, 1, LANES), lambda b, h, c: (b, h, 0, 0))],
        out_shape=[jax.ShapeDtypeStruct((batch * seq, heads * dv), BF16),
                   jax.ShapeDtypeStruct((batch, heads, dv, dk), F32),
                   jax.ShapeDtypeStruct((batch, heads, 1, dk), F32),
                   jax.ShapeDtypeStruct((batch, heads, 1, LANES), F32)],
        compiler_params=_params(("parallel", "parallel", "arbitrary"), blocks,
                                8 * _nbytes((L, L), F32) + 4 * _nbytes((L, dv), F32)
                                + 3 * _nbytes((dv, dk), F32)),
        name="mlstm_chunkwise",
    )(p1, p1, p1, p2, norm_w, g, b_loc, i_row, b_row)


def _mlstm_step_body(q_ref, k_ref, v_ref, og_ref, w_ref, g_ref, c_ref, n_ref, m_ref,
                     y_ref, co_ref, no_ref, mo_ref, *, heads, dk, dv, k_scale):
    g = g_ref[0]
    eye = (lax.broadcasted_iota(jnp.int32, (dv, dv), 0)
           == lax.broadcasted_iota(jnp.int32, (dv, dv), 1)).astype(BF16)
    lane = lax.broadcasted_iota(jnp.int32, (1, LANES), 1)
    m_out = jnp.zeros((1, LANES), F32)
    for h in range(heads):
        q = q_ref[0, :, h * dk:(h + 1) * dk].astype(F32)
        kf = k_ref[0, :, h * dk:(h + 1) * dk].astype(F32) * k_scale
        v = v_ref[0, :, h * dv:(h + 1) * dv].astype(F32)
        ig = g[:, MIG_LANE0 + h:MIG_LANE0 + h + 1]
        lf = g[:, MLF_LANE0 + h:MLF_LANE0 + h + 1]
        m_prev = m_ref[0][:, h:h + 1]
        c_state = c_ref[0, h]
        n_state = n_ref[0, h]
        inter = lf + m_prev
        m_t = jnp.maximum(inter, ig)
        w_intra = jnp.exp(ig - m_t)
        w_inter = jnp.exp(inter - m_t)
        a = w_intra * jnp.sum(q * kf, axis=1, keepdims=True)
        q_rows = jnp.broadcast_to(q, (HEAD_ROWS, dk)).astype(BF16)
        cq = lax.dot_general(q_rows, c_state.astype(BF16), NT_DIMS, preferred_element_type=F32)[0:1]
        num = a * v + w_inter * cq
        den = a + w_inter * jnp.sum(q * n_state, axis=1, keepdims=True)
        hcur = num / jnp.maximum(jnp.abs(den), jnp.exp(-m_t))
        y_ref[0, :, h * dv:(h + 1) * dv] = _head_norm_gate(
            hcur, w_ref[:, h * dv:(h + 1) * dv], og_ref[0, :, h * dv:(h + 1) * dv]).astype(y_ref.dtype)
        v_rows = jnp.broadcast_to(v, (HEAD_ROWS, dv)).astype(BF16)
        v_col = lax.dot_general(eye, v_rows, NT_DIMS, preferred_element_type=F32)[:, 0:1]
        kg = w_intra * kf
        co_ref[0, h] = w_inter * c_state + v_col * kg
        no_ref[0, h] = w_inter * n_state + kg
        m_out = jnp.where(lane == h, m_t, m_out)
    mo_ref[0] = m_out


def _mlstm_step(q, k, v, og, norm_w, g, c0, n0, m0, heads, dk, dv, k_scale):
    db = q.shape[0]
    row = lambda b: (b, 0, 0)
    st = lambda b: (b, 0, 0, 0)
    blocks = 2 * _nbytes((heads, dv, dk), F32) + 4 * _nbytes((1, heads * dv), F32)
    return pl.pallas_call(
        functools.partial(_mlstm_step_body, heads=heads, dk=dk, dv=dv, k_scale=k_scale),
        grid=(db,),
        in_specs=[pl.BlockSpec((1, 1, heads * dk), row), pl.BlockSpec((1, 1, heads * dk), row),
                  pl.BlockSpec((1, 1, heads * dv), row), pl.BlockSpec((1, 1, heads * dv), row),
                  pl.BlockSpec((1, heads * dv), lambda b: (0, 0)),
                  pl.BlockSpec((1, 1, LANES), row),
                  pl.BlockSpec((1, heads, dv, dk), st), pl.BlockSpec((1, heads, 1, dk), st),
                  pl.BlockSpec((1, 1, heads), row)],
        out_specs=[pl.BlockSpec((1, 1, heads * dv), row),
                   pl.BlockSpec((1, heads, dv, dk), st), pl.BlockSpec((1, heads, 1, dk), st),
                   pl.BlockSpec((1, 1, LANES), row)],
        out_shape=[jax.ShapeDtypeStruct((db, 1, heads * dv), BF16),
                   jax.ShapeDtypeStruct((db, heads, dv, dk), F32),
                   jax.ShapeDtypeStruct((db, heads, 1, dk), F32),
                   jax.ShapeDtypeStruct((db, 1, LANES), F32)],
        compiler_params=_params(("parallel",), blocks, 6 * _nbytes((dv, dk), F32)),
        name="mlstm_step",
    )(q, k, v, og, norm_w, g, c0, n0, m0)


def _bf16_cols(w, *ranges):
    return jnp.concatenate([w[:, a:b].astype(BF16) for a, b in ranges], axis=1)


def kernel(x_prompt, x_sample, mem_prompt, cache_fox_k, cache_fox_v, cache_fox_logf, state_mlstm_C, state_mlstm_n, state_mlstm_m, cache_mem_k, cache_mem_v, page_table, attn_norm_w, w_in, fox_f_bias, m_i_bias, m_f_bias, m_norm_w, mem_norm_w, w_mem_kv, w_br_fox, w_br_m, w_br_mem, w_out, ffn_norm_w, w_up, w_down, final_norm_w):
    batch, seq, d_model = x_prompt.shape
    dec_batch, dec_seq, _ = x_sample.shape
    depth = w_in.shape[0]
    assert depth == 1 and dec_seq == 1
    fox_heads, fox_dh = cache_fox_k.shape[3], cache_fox_k.shape[4]
    m_heads, m_dv, m_dk = state_mlstm_C.shape[2:]
    n_mem, mem_heads, mem_dh = cache_mem_k.shape[2:]
    fox_w, mqk_w, mv_w, mem_w = fox_heads * fox_dh, m_heads * m_dk, m_heads * m_dv, mem_heads * mem_dh
    assert fox_heads == MIG_LANE0 and m_heads == MLF_LANE0 - MIG_LANE0
    tp, ts = batch * seq, dec_batch * dec_seq
    k_scale = m_dk ** -0.5

    sizes = (fox_w, fox_w, fox_w, fox_heads, mqk_w, mqk_w, mv_w, m_heads, m_heads, mv_w, mem_w,
             3 * d_model)
    offs = [0]
    for s in sizes:
        offs.append(offs[-1] + s)
    seg = lambda i: (offs[i], offs[i + 1])

    l = 0
    w = w_in[l]
    wa_cols, wb_cols, wc_cols = 3 * fox_w, 2 * mqk_w + mv_w, mv_w + mem_w + 3 * d_model
    w_a = _shift_cast(w, offs[0], wa_cols, 1024, 512)
    w_b = _shift_cast(w, offs[4], wb_cols, 1024, 512)
    w_c = _shift_cast(w, offs[9], wc_cols, 1024, 512)
    PB_MQ, PB_MK, PB_MV = 0, mqk_w, 2 * mqk_w
    PC_OG, PC_QQ, PC_GATES = 0, mv_w, mv_w + mem_w
    n_small = fox_heads + 2 * m_heads
    w_sm = jnp.concatenate([_bf16_cols(w, seg(3), seg(7), seg(8)),
                            jnp.zeros((d_model, LANES - n_small), BF16)], axis=1)
    bias_sm = jnp.concatenate([fox_f_bias[l], m_i_bias[l], m_f_bias[l],
                               jnp.zeros((LANES - n_small,), F32)]).reshape(1, LANES).astype(F32)
    w_memkv_b = w_mem_kv[l].astype(BF16)
    w_brf, w_brm, w_brq = w_br_fox[l].astype(BF16), w_br_m[l].astype(BF16), w_br_mem[l].astype(BF16)
    w_out_b, w_up_b, w_down_b = w_out[l].astype(BF16), w_up[l].astype(BF16), w_down[l].astype(BF16)
    d_ff = w_up_b.shape[1]
    m_norm = m_norm_w[l].reshape(1, mv_w).astype(F32)

    def wide_tn(*col_counts):
        return 1024 if all(c % 1024 == 0 for c in col_counts) else 512

    def project(x2d, tm):
        h = _rmsnorm(x2d, attn_norm_w[l], BF16, min(tm, 256))
        tn = 512
        fq = _matmul(h, w_a, n_cols=fox_w, col_off=0, out_dtype=BF16, tm=tm, tn=tn)
        fk = _matmul(h, w_a, n_cols=fox_w, col_off=fox_w, out_dtype=F32, tm=tm, tn=tn)
        fv = _matmul(h, w_a, n_cols=fox_w, col_off=2 * fox_w, out_dtype=F32, tm=tm, tn=tn)
        pb = _matmul(h, w_b, n_cols=wb_cols, out_dtype=BF16, tm=tm, tn=wide_tn(wb_cols))
        pc = _matmul(h, w_c, n_cols=wc_cols, out_dtype=BF16, tm=tm, tn=wide_tn(wc_cols, PC_QQ, PC_GATES),
                     epilogue="sigmoid", plain_cols=(PC_QQ, PC_GATES))
        sp = _matmul(h, w_sm, n_cols=LANES, out_dtype=F32, tm=tm, tn=LANES)
        return fq, fk, fv, pb, pc, sp

    def mix_and_mlp(x2d, y_fox, y_m, y_mem, pc, tm):
        merged = _merge(y_fox, y_m, y_mem, w_brf, w_brm, w_brq, pc, PC_GATES, min(tm, 512), 512)
        x1 = _matmul(merged, w_out_b, n_cols=d_model, out_dtype=F32, tm=tm, tn=512, resid=x2d)
        h2 = _rmsnorm(x1, ffn_norm_w[l], BF16, min(tm, 256))
        up = _matmul(h2, w_up_b, n_cols=d_ff, out_dtype=BF16, tm=tm, tn=wide_tn(d_ff), epilogue="relu2")
        x2 = _matmul(up, w_down_b, n_cols=d_model, out_dtype=F32, tm=min(tm, 512), tn=256, resid=x1)
        return _rmsnorm(x2, final_norm_w, F32, min(tm, 256))

    xp = x_prompt.reshape(tp, d_model)
    fq, fk, fv, pb, pc, sp = project(xp, 1024)
    g, c_glob, b_loc = _gates(sp.reshape(batch, seq, LANES), bias_sm, M_CHUNK)
    ck = c_glob[:, :, FOX_LANE0:FOX_LANE0 + fox_heads].transpose(0, 2, 1).reshape(batch, fox_heads, 1, seq)
    i_row = g[:, :, MIG_LANE0:MIG_LANE0 + m_heads].transpose(0, 2, 1).reshape(batch, m_heads, 1, seq)
    b_row = b_loc[:, :, MLF_LANE0:MLF_LANE0 + m_heads].transpose(0, 2, 1).reshape(batch, m_heads, 1, seq)

    y_fox = _fox_prompt(fq, 0, fk, fv, c_glob, ck, batch, seq, fox_heads, fox_dh, 512)
    y_m, p_c, p_n, p_m = _mlstm_prompt(pb, PB_MQ, PB_MK, PB_MV, pc, PC_OG, m_norm, g, b_loc, i_row, b_row,
                                       batch, seq, m_heads, m_dk, m_dv, k_scale)
    mem_h = _rmsnorm(mem_prompt.reshape(batch * n_mem, d_model), mem_norm_w[l], BF16, 256)
    mem_k = _matmul(mem_h, w_memkv_b, n_cols=mem_w, col_off=0, out_dtype=F32, tm=batch * n_mem, tn=512)
    mem_v = _matmul(mem_h, w_memkv_b, n_cols=mem_w, col_off=mem_w, out_dtype=F32, tm=batch * n_mem, tn=512)
    y_mem = _mem_prompt(pc, PC_QQ, mem_k, mem_v, batch, seq, n_mem, mem_heads, mem_dh, 512)
    y_prompt = mix_and_mlp(xp, y_fox, y_m, y_mem, pc, 1024).reshape(batch, seq, d_model)

    xs = x_sample.reshape(ts, d_model)
    sfq, sfk, sfv, spb, spc, ssp = project(xs, ts)
    sg = _gates_single(ssp, bias_sm)
    s_flf = sg[:, FOX_LANE0:FOX_LANE0 + fox_heads]
    per_head = lambda a: a.reshape(ts, fox_heads, fox_dh)
    sy_fox = _fox_decode(
        per_head(sfq), per_head(sfk), per_head(sfv), s_flf.reshape(ts, fox_heads, 1),
        cache_fox_k[l], cache_fox_v[l], cache_fox_logf[l].transpose(0, 2, 1), page_table, 8).reshape(ts, fox_w)
    row3 = lambda a: a.reshape(ts, 1, a.shape[-1])
    sy_m, s_c, s_n, s_m = _mlstm_step(
        row3(spb[:, PB_MQ:PB_MQ + mqk_w]), row3(spb[:, PB_MK:PB_MK + mqk_w]), row3(spb[:, PB_MV:PB_MV + mv_w]),
        row3(spc[:, PC_OG:PC_OG + mv_w]), m_norm, row3(sg),
        state_mlstm_C[l], state_mlstm_n[l].reshape(ts, m_heads, 1, m_dk),
        state_mlstm_m[l].reshape(ts, 1, m_heads), m_heads, m_dk, m_dv, k_scale)
    sy_mem = _mem_decode(spc[:, PC_QQ:PC_QQ + mem_w], cache_mem_k[l].reshape(ts, n_mem, mem_w),
                         cache_mem_v[l].reshape(ts, n_mem, mem_w), mem_heads, mem_dh).reshape(ts, mem_w)
    y_sample = mix_and_mlp(xs, sy_fox, sy_m.reshape(ts, mv_w), sy_mem, spc, ts).reshape(dec_batch, dec_seq, d_model)

    lead = lambda a, shape: a.reshape((1,) + shape)
    return (
        y_prompt, y_sample,
        lead(fk, (batch, seq, fox_heads, fox_dh)), lead(fv, (batch, seq, fox_heads, fox_dh)),
        lead(g[:, :, FOX_LANE0:FOX_LANE0 + fox_heads], (batch, seq, fox_heads)),
        lead(p_c, (batch, m_heads, m_dv, m_dk)), lead(p_n, (batch, m_heads, m_dk)),
        lead(p_m[:, :, 0, 0], (batch, m_heads)),
        lead(mem_k, (batch, n_mem, mem_heads, mem_dh)), lead(mem_v, (batch, n_mem, mem_heads, mem_dh)),
        lead(sfk, (dec_batch, dec_seq, fox_heads, fox_dh)), lead(sfv, (dec_batch, dec_seq, fox_heads, fox_dh)),
        lead(s_flf, (dec_batch, dec_seq, fox_heads)),
        lead(s_c, (dec_batch, m_heads, m_dv, m_dk)), lead(s_n, (dec_batch, m_heads, m_dk)),
        lead(s_m[:, 0, :m_heads], (dec_batch, m_heads)),
    )
```

```python
import functools

import jax
import jax.numpy as jnp
from jax import lax
from jax.experimental import pallas as pl
from jax.experimental.pallas import tpu as pltpu

F32 = jnp.float32
BF16 = jnp.bfloat16
HIGHEST = lax.Precision.HIGHEST

EPS = 1e-6
PAGE_SIZE = 128
LANES = 128
VMEM_LIMIT_CAP = 60000 * 1024
VMEM_SLACK = 6 * 1024 * 1024

FOX_LANE0 = 0
MIG_LANE0 = 16
MLF_LANE0 = 24
M_CHUNK = 256
NT_DIMS = (((1,), (1,)), ((), ()))


def _nbytes(shape, dtype):
    n = 1
    for s in shape:
        n *= s
    return n * jnp.dtype(dtype).itemsize


def _params(semantics, block_bytes, extra_bytes=0):
    limit = min(2 * block_bytes + extra_bytes + VMEM_SLACK, VMEM_LIMIT_CAP)
    return pltpu.CompilerParams(dimension_semantics=semantics, vmem_limit_bytes=int(limit))


def _log_sigmoid(x):
    return jnp.minimum(x, 0.0) - jnp.log1p(jnp.exp(-jnp.abs(x)))


def _sigmoid(x):
    return 1.0 / (1.0 + jnp.exp(-x))


def _rmsnorm_body(x_ref, w_ref, o_ref):
    x = x_ref[...].astype(F32)
    y = x * lax.rsqrt(jnp.mean(x * x, axis=-1, keepdims=True) + EPS)
    o_ref[...] = (y * w_ref[...]).astype(o_ref.dtype)


def _rmsnorm(x, w, out_dtype, tm):
    m, d = x.shape
    blocks = _nbytes((tm, d), x.dtype) + _nbytes((tm, d), out_dtype) + _nbytes((1, d), F32)
    return pl.pallas_call(
        _rmsnorm_body,
        grid=(m // tm,),
        in_specs=[pl.BlockSpec((tm, d), lambda i: (i, 0)),
                  pl.BlockSpec((1, d), lambda i: (0, 0))],
        out_specs=pl.BlockSpec((tm, d), lambda i: (i, 0)),
        out_shape=jax.ShapeDtypeStruct((m, d), out_dtype),
        compiler_params=_params(("parallel",), blocks, _nbytes((tm, d), F32) * 2),
        name="rmsnorm",
    )(x, w.reshape(1, d).astype(F32))


TAIL_ROWS = 32


def _window_cast_body(*refs, shift):
    a_ref, o_ref = refs[0], refs[-1]
    a = a_ref[...]
    if shift:
        wide = jnp.concatenate([a, refs[1][...]], axis=0)
        a = wide[shift:shift + a.shape[0], :]
    o_ref[...] = a.T.astype(o_ref.dtype)


def _window_cast(wt, layer, row0, n_rows, tn, tk):
    kdim = wt.shape[2]
    base = row0 // tn * tn
    shift = row0 - base
    assert kdim % tk == 0 and n_rows % tn == 0 and tn % TAIL_ROWS == 0
    assert shift % 8 == 0 and shift <= TAIL_ROWS and row0 + n_rows <= wt.shape[1]
    jb = base // tn
    in_specs = [pl.BlockSpec((None, tn, tk), lambda j, i: (layer, jb + j, i))]
    args = [wt]
    if shift:
        in_specs.append(pl.BlockSpec((None, TAIL_ROWS, tk),
                                     lambda j, i: (layer, (jb + j + 1) * (tn // TAIL_ROWS), i)))
        args.append(wt)
    blocks = _nbytes((tn + TAIL_ROWS, tk), F32) + _nbytes((tk, tn), BF16)
    return pl.pallas_call(
        functools.partial(_window_cast_body, shift=shift),
        grid=(n_rows // tn, kdim // tk),
        in_specs=in_specs,
        out_specs=pl.BlockSpec((tk, tn), lambda j, i: (i, j)),
        out_shape=jax.ShapeDtypeStruct((kdim, n_rows), BF16),
        compiler_params=_params(("parallel", "parallel"), blocks, 3 * _nbytes((tn + TAIL_ROWS, tk), F32)),
        name="window_cast",
    )(*args)


def _gate_weight_body(f_ref, m_ref, o_ref):
    rows = jnp.concatenate(
        [f_ref[...], m_ref[...],
         jnp.zeros((LANES - f_ref.shape[0] - m_ref.shape[0], f_ref.shape[1]), F32)], axis=0)
    o_ref[...] = rows.T.astype(o_ref.dtype)


def _gate_weights(wt, layer, row_fox, row_m, n_fox, n_m):
    assert row_fox % n_fox == 0 and row_m % n_m == 0 and n_fox % 8 == 0 and n_m % 8 == 0
    kdim = wt.shape[2]
    return pl.pallas_call(
        _gate_weight_body,
        grid=(1,),
        in_specs=[pl.BlockSpec((None, n_fox, kdim), lambda i: (layer, row_fox // n_fox, 0)),
                  pl.BlockSpec((None, n_m, kdim), lambda i: (layer, row_m // n_m, 0))],
        out_specs=pl.BlockSpec((kdim, LANES), lambda i: (0, 0)),
        out_shape=jax.ShapeDtypeStruct((kdim, LANES), BF16),
        name="gate_weights",
    )(wt, wt)


def _mm_body(*refs, nk, epilogue, plain_blocks, has_resid):
    x_ref, w_ref = refs[0], refs[1]
    r_ref = refs[2] if has_resid else None
    o_ref = refs[2 + has_resid]
    acc_ref = refs[3 + has_resid] if nk > 1 else None
    j = pl.program_id(1)
    part = jnp.dot(x_ref[...], w_ref[...], preferred_element_type=F32)

    def finish(acc):
        if epilogue == "sigmoid":
            gated = _sigmoid(acc)
            if plain_blocks is not None:
                gated = jnp.where((j >= plain_blocks[0]) & (j < plain_blocks[1]), acc, gated)
            acc = gated
        elif epilogue == "relu2":
            r = jnp.maximum(acc, 0.0)
            acc = r * r
        if has_resid:
            acc = r_ref[...] + acc
        o_ref[...] = acc.astype(o_ref.dtype)

    if nk == 1:
        finish(part)
    else:
        k = pl.program_id(2)

        @pl.when(k == 0)
        def _():
            acc_ref[...] = part

        @pl.when(k > 0)
        def _():
            acc_ref[...] += part

        @pl.when(k == nk - 1)
        def _():
            finish(acc_ref[...])


def _matmul(x, w, *, n_cols, col_off=0, out_dtype, tm, tn, tk=None, epilogue=None, plain_cols=None,
            resid=None):
    m, kdim = x.shape
    tk = kdim if tk is None else tk
    nk = kdim // tk
    assert m % tm == 0 and n_cols % tn == 0 and col_off % tn == 0 and kdim % tk == 0
    joff = col_off // tn
    plain_blocks = None
    if plain_cols is not None:
        assert epilogue == "sigmoid" and plain_cols[0] % tn == 0 and plain_cols[1] % tn == 0
        plain_blocks = (plain_cols[0] // tn, plain_cols[1] // tn)
    in_specs = [pl.BlockSpec((tm, tk), lambda i, j, k: (i, k)),
                pl.BlockSpec((tk, tn), lambda i, j, k: (k, j + joff))]
    args = [x, w]
    blocks = _nbytes((tm, tk), x.dtype) + _nbytes((tk, tn), w.dtype) + _nbytes((tm, tn), out_dtype)
    if resid is not None:
        in_specs.append(pl.BlockSpec((tm, tn), lambda i, j, k: (i, j)))
        args.append(resid)
        blocks += _nbytes((tm, tn), resid.dtype)
    scratch = [pltpu.VMEM((tm, tn), F32)] if nk > 1 else []
    extra = _nbytes((tm, tn), F32) * (3 if nk > 1 else 2)
    return pl.pallas_call(
        functools.partial(_mm_body, nk=nk, epilogue=epilogue, plain_blocks=plain_blocks,
                          has_resid=resid is not None),
        grid=(m // tm, n_cols // tn, nk),
        in_specs=in_specs,
        out_specs=pl.BlockSpec((tm, tn), lambda i, j, k: (i, j)),
        out_shape=jax.ShapeDtypeStruct((m, n_cols), out_dtype),
        scratch_shapes=scratch,
        compiler_params=_params(("parallel", "parallel", "arbitrary"), blocks, extra),
        name="matmul_" + (epilogue or "plain"),
    )(*args)


def _merge_body(yf_ref, ym_ref, yq_ref, wf_ref, wm_ref, wq_ref, g0_ref, g1_ref, g2_ref, o_ref):
    acc = g0_ref[...].astype(F32) * jnp.dot(yf_ref[...], wf_ref[...], preferred_element_type=F32)
    acc += g1_ref[...].astype(F32) * jnp.dot(ym_ref[...], wm_ref[...], preferred_element_type=F32)
    acc += g2_ref[...].astype(F32) * jnp.dot(yq_ref[...], wq_ref[...], preferred_element_type=F32)
    o_ref[...] = acc.astype(o_ref.dtype)


def _merge(y_fox, y_m, y_mem, w_fox, w_m, w_mem, gates, gate_off, tm, tn):
    t = y_fox.shape[0]
    d = w_fox.shape[1]
    goff = gate_off // tn
    nd = d // tn
    kf, km, kq = y_fox.shape[1], y_m.shape[1], y_mem.shape[1]
    blocks = (_nbytes((tm, kf + km + kq), BF16) + _nbytes((kf + km + kq, tn), BF16)
              + 4 * _nbytes((tm, tn), BF16))
    return pl.pallas_call(
        _merge_body,
        grid=(t // tm, nd),
        in_specs=[pl.BlockSpec((tm, kf), lambda i, j: (i, 0)),
                  pl.BlockSpec((tm, km), lambda i, j: (i, 0)),
                  pl.BlockSpec((tm, kq), lambda i, j: (i, 0)),
                  pl.BlockSpec((kf, tn), lambda i, j: (0, j)),
                  pl.BlockSpec((km, tn), lambda i, j: (0, j)),
                  pl.BlockSpec((kq, tn), lambda i, j: (0, j)),
                  pl.BlockSpec((tm, tn), lambda i, j: (i, goff + j)),
                  pl.BlockSpec((tm, tn), lambda i, j: (i, goff + nd + j)),
                  pl.BlockSpec((tm, tn), lambda i, j: (i, goff + 2 * nd + j))],
        out_specs=pl.BlockSpec((tm, tn), lambda i, j: (i, j)),
        out_shape=jax.ShapeDtypeStruct((t, d), BF16),
        compiler_params=_params(("parallel", "parallel"), blocks, 4 * _nbytes((tm, tn), F32)),
        name="gated_merge",
    )(y_fox, y_m, y_mem, w_fox, w_m, w_mem, gates, gates, gates)


def _gate_values(x):
    lane = lax.broadcasted_iota(jnp.int32, (1, LANES), 1)
    is_ig = (lane >= MIG_LANE0) & (lane < MLF_LANE0)
    return jnp.where(is_ig, x, _log_sigmoid(x))


def _gate_only_body(sp_ref, bias_ref, g_ref):
    g_ref[...] = _gate_values(sp_ref[...] + bias_ref[...])


def _gates_single(sp, bias):
    t = sp.shape[0]
    spec = pl.BlockSpec((t, LANES), lambda i: (0, 0))
    return pl.pallas_call(
        _gate_only_body,
        grid=(1,),
        in_specs=[spec, pl.BlockSpec((1, LANES), lambda i: (0, 0))],
        out_specs=spec,
        out_shape=jax.ShapeDtypeStruct((t, LANES), F32),
        name="small_gates_single",
    )(sp, bias)


def _gate_body(sp_ref, bias_ref, g_ref, cg_ref, bl_ref, *, seq, chunk):
    rows = chunk
    r = lax.broadcasted_iota(jnp.int32, (rows, rows), 0)
    c = lax.broadcasted_iota(jnp.int32, (rows, rows), 1)
    tri = (r >= c).astype(F32)
    carry = jnp.zeros((1, LANES), F32)
    for t in range(seq // rows):
        sl = slice(t * rows, (t + 1) * rows)
        g = _gate_values(sp_ref[0, sl, :] + bias_ref[...])
        g_ref[0, sl, :] = g
        loc = jnp.dot(tri, g, precision=HIGHEST, preferred_element_type=F32)
        bl_ref[0, sl, :] = loc
        cg_ref[0, sl, :] = loc + carry
        carry = carry + loc[rows - 1:rows, :]


def _gates(sp, bias, chunk):
    b, s, _ = sp.shape
    spec = pl.BlockSpec((1, s, LANES), lambda i: (i, 0, 0))
    shape = jax.ShapeDtypeStruct((b, s, LANES), F32)
    return pl.pallas_call(
        functools.partial(_gate_body, seq=s, chunk=chunk),
        grid=(b,),
        in_specs=[spec, pl.BlockSpec((1, LANES), lambda i: (0, 0))],
        out_specs=[spec, spec, spec],
        out_shape=[shape, shape, shape],
        compiler_params=_params(("parallel",), 4 * _nbytes((s, LANES), F32), _nbytes((s, LANES), F32)),
        name="small_gates",
    )(sp, bias)


def _lane_column(tile, lane_index):
    lane = lax.broadcasted_iota(jnp.int32, (1, LANES), 1)
    return jnp.sum(jnp.where(lane == lane_index, tile, 0.0), axis=1, keepdims=True)


def _ride_cast(w, layer, n_steps, step_of):
    rows, cols = w.shape[1:]
    slab = rows // n_steps
    assert rows % n_steps == 0 and slab % 16 == 0
    in_spec = pl.BlockSpec((None, slab, cols), lambda *g: (layer, step_of(*g), 0))
    out_spec = pl.BlockSpec((slab, cols), lambda *g: (step_of(*g), 0))
    block_bytes = _nbytes((slab, cols), F32) + _nbytes((slab, cols), BF16)
    return in_spec, out_spec, jax.ShapeDtypeStruct((rows, cols), BF16), block_bytes


def _fox_prompt_body(q_ref, k_ref, v_ref, cq_ref, ck_ref, wsrc_ref, o_ref, wdst_ref, kb_sc, vb_sc, *,
                     scale, tq):
    h = pl.program_id(1)
    seq = q_ref.shape[0]
    wdst_ref[...] = wsrc_ref[...].astype(wdst_ref.dtype)
    kb_sc[...] = k_ref[...].astype(BF16)
    vb_sc[...] = v_ref[...].astype(BF16)
    r = lax.broadcasted_iota(jnp.int32, (tq, tq), 0)
    c = lax.broadcasted_iota(jnp.int32, (tq, tq), 1)
    causal = c <= r
    for i in range(seq // tq):
        lo, hi = i * tq, (i + 1) * tq
        q = q_ref[lo:hi, :]
        cq = _lane_column(cq_ref[0, lo:hi, :], h)

        def scores(a, b):
            s = lax.dot_general(q, kb_sc[a:b, :], NT_DIMS, preferred_element_type=F32) * scale
            return s + cq - ck_ref[0, 0, :, a:b]

        s_d = jnp.where(causal, scores(lo, hi), -jnp.inf)
        m = jnp.max(s_d, axis=1, keepdims=True)
        if i > 0:
            s_o = scores(0, lo)
            m = jnp.maximum(m, jnp.max(s_o, axis=1, keepdims=True))
        p_d = jnp.exp(s_d - m)
        l = jnp.sum(p_d, axis=1, keepdims=True)
        acc = jnp.dot(p_d.astype(BF16), vb_sc[lo:hi, :], preferred_element_type=F32)
        if i > 0:
            p_o = jnp.exp(s_o - m)
            l = l + jnp.sum(p_o, axis=1, keepdims=True)
            acc = acc + jnp.dot(p_o.astype(BF16), vb_sc[0:lo, :], preferred_element_type=F32)
        o_ref[lo:hi, :] = (acc / l).astype(o_ref.dtype)


def _fox_prompt(qsrc, q_col0, k, v, cq, ck, batch, seq, heads, dh, tq, cast_w, cast_layer):
    qoff = q_col0 // dh
    w_in_spec, w_out_spec, w_shape, w_bytes = _ride_cast(
        cast_w, cast_layer, batch * heads, lambda b, h: b * heads + h)
    blocks = (2 * _nbytes((seq, dh), BF16) + 2 * _nbytes((seq, dh), F32)
              + _nbytes((seq, LANES), F32) + _nbytes((8, seq), F32) + w_bytes)
    return pl.pallas_call(
        functools.partial(_fox_prompt_body, scale=dh ** -0.5, tq=tq),
        grid=(batch, heads),
        in_specs=[pl.BlockSpec((seq, dh), lambda b, h: (b, qoff + h)),
                  pl.BlockSpec((seq, dh), lambda b, h: (b, h)),
                  pl.BlockSpec((seq, dh), lambda b, h: (b, h)),
                  pl.BlockSpec((1, seq, LANES), lambda b, h: (b, 0, 0)),
                  pl.BlockSpec((1, 1, 1, seq), lambda b, h: (b, h, 0, 0)),
                  w_in_spec],
        out_specs=[pl.BlockSpec((seq, dh), lambda b, h: (b, h)), w_out_spec],
        out_shape=[jax.ShapeDtypeStruct((batch * seq, heads * dh), BF16), w_shape],
        scratch_shapes=[pltpu.VMEM((seq, dh), BF16), pltpu.VMEM((seq, dh), BF16)],
        compiler_params=_params(("parallel", "parallel"), blocks, 8 * _nbytes((tq, seq), F32)),
        name="fox_prompt_attention",
    )(qsrc, k, v, cq, ck, cast_w)


def _mem_prompt_body(q_ref, k_ref, v_ref, o_ref, *, scale):
    s = lax.dot_general(q_ref[...], k_ref[...].astype(BF16), NT_DIMS,
                        preferred_element_type=F32) * scale
    m = jnp.max(s, axis=1, keepdims=True)
    p = jnp.exp(s - m)
    p = p / jnp.sum(p, axis=1, keepdims=True)
    o_ref[...] = jnp.dot(p.astype(BF16), v_ref[...].astype(BF16),
                         preferred_element_type=F32).astype(o_ref.dtype)


def _mem_prompt(qsrc, q_col0, k, v, batch, seq, n_mem, heads, dh, tq):
    nq = seq // tq
    qoff = q_col0 // dh
    blocks = 2 * _nbytes((tq, dh), BF16) + 2 * _nbytes((n_mem, dh), F32)
    return pl.pallas_call(
        functools.partial(_mem_prompt_body, scale=dh ** -0.5),
        grid=(batch, heads, nq),
        in_specs=[pl.BlockSpec((tq, dh), lambda b, h, i: (b * nq + i, qoff + h)),
                  pl.BlockSpec((n_mem, dh), lambda b, h, i: (b, h)),
                  pl.BlockSpec((n_mem, dh), lambda b, h, i: (b, h))],
        out_specs=pl.BlockSpec((tq, dh), lambda b, h, i: (b * nq + i, h)),
        out_shape=jax.ShapeDtypeStruct((batch * seq, heads * dh), BF16),
        compiler_params=_params(("parallel", "parallel", "parallel"), blocks,
                                4 * _nbytes((tq, n_mem), F32) + _nbytes((tq, dh), F32)),
        name="memory_prompt_attention",
    )(qsrc, k, v)


HEAD_ROWS = 16


def _head_masks(heads, dh):
    assert heads <= HEAD_ROWS
    row = lax.broadcasted_iota(jnp.int32, (HEAD_ROWS, heads * dh), 0)
    col = lax.broadcasted_iota(jnp.int32, (HEAD_ROWS, heads * dh), 1)
    return (col >= row * dh) & (col < (row + 1) * dh)


def _block_diag(row, mask):
    return jnp.where(mask, jnp.broadcast_to(row.astype(F32), mask.shape), 0.0)


def _decode_block(k_blk, v_blk, bias, qbd, scale, m_sc, l_sc, acc_sc):
    s = lax.dot_general(qbd, k_blk.astype(BF16), NT_DIMS, preferred_element_type=F32) * scale
    if bias is not None:
        s = s + bias
    m_prev = m_sc[...]
    m_new = jnp.maximum(m_prev, jnp.max(s, axis=1, keepdims=True))
    alpha = jnp.exp(m_prev - m_new)
    p = jnp.exp(s - m_new)
    l_sc[...] = alpha * l_sc[...] + jnp.sum(p, axis=1, keepdims=True)
    acc_sc[...] = alpha * acc_sc[...] + jnp.dot(p.astype(BF16), v_blk.astype(BF16),
                                               preferred_element_type=F32)
    m_sc[...] = m_new


def _decode_output(mask, l_sc, acc_sc):
    return jnp.sum(jnp.where(mask, acc_sc[...] / l_sc[...], 0.0), axis=0, keepdims=True)


def _split_bf16(x):
    hi = x.astype(BF16)
    r1 = x - hi.astype(F32)
    mid = r1.astype(BF16)
    lo = (r1 - mid.astype(F32)).astype(BF16)
    return hi, mid, lo


def _fox_decode_body(pt_ref, q_ref, kn_ref, vn_ref, lfn_ref, *refs, heads, dh, pages_per_step, scale):
    npg = pages_per_step
    k_refs = refs[0:npg]
    v_refs = refs[npg:2 * npg]
    lf_refs = refs[2 * npg:3 * npg]
    o_ref = refs[3 * npg]
    spread_sc, m_sc, l_sc, acc_sc, carry_sc = refs[3 * npg + 1:]
    j = pl.program_id(1)
    flat = PAGE_SIZE * heads

    @pl.when(j == 0)
    def _():
        m_sc[...] = jnp.sum(q_ref[0].astype(F32) * kn_ref[0], axis=1, keepdims=True) * scale
        l_sc[...] = jnp.ones(l_sc.shape, F32)
        acc_sc[...] = vn_ref[0]
        carry_sc[...] = lfn_ref[0]
        key_of = lax.broadcasted_iota(jnp.int32, (PAGE_SIZE, flat), 1) // heads
        spread_sc[...] = (key_of == lax.broadcasted_iota(jnp.int32, (PAGE_SIZE, flat), 0)).astype(BF16)

    r = lax.broadcasted_iota(jnp.int32, (PAGE_SIZE, PAGE_SIZE), 0)
    c = lax.broadcasted_iota(jnp.int32, (PAGE_SIZE, PAGE_SIZE), 1)
    later = (r > c).astype(F32)
    carry = carry_sc[...]
    pieces = []
    for t in range(npg):
        lf = lf_refs[t][0]
        pieces.extend(_split_bf16(carry + jnp.dot(lf, later, precision=HIGHEST,
                                                  preferred_element_type=F32)))
        carry = carry + jnp.sum(lf, axis=1, keepdims=True)
    carry_sc[...] = carry
    bias_flat = jnp.dot(jnp.concatenate(pieces, axis=0), spread_sc[...], preferred_element_type=F32)

    own_head = (lax.broadcasted_iota(jnp.int32, (heads, flat), 1) % heads
                == lax.broadcasted_iota(jnp.int32, (heads, flat), 0))
    q = q_ref[0]
    scores = []
    for t in range(npg):
        b0 = 3 * heads * t
        bias = (bias_flat[b0:b0 + heads] + bias_flat[b0 + heads:b0 + 2 * heads]
                + bias_flat[b0 + 2 * heads:b0 + 3 * heads])
        k2 = k_refs[t][0].reshape(flat, dh).astype(BF16)
        s = lax.dot_general(q, k2, NT_DIMS, preferred_element_type=F32) * scale + bias
        scores.append(jnp.where(own_head, s, -jnp.inf))
    m_prev = m_sc[...]
    m_new = m_prev
    for s in scores:
        m_new = jnp.maximum(m_new, jnp.max(s, axis=1, keepdims=True))
    alpha = jnp.exp(m_prev - m_new)
    l_new = alpha * l_sc[...]
    acc = alpha * acc_sc[...]
    for t in range(npg):
        p = jnp.exp(scores[t] - m_new)
        l_new = l_new + jnp.sum(p, axis=1, keepdims=True)
        v2 = v_refs[t][0].reshape(flat, dh).astype(BF16)
        acc = acc + jnp.dot(p.astype(BF16), v2, preferred_element_type=F32)
    l_sc[...] = l_new
    acc_sc[...] = acc
    m_sc[...] = m_new

    @pl.when(j == pl.num_programs(1) - 1)
    def _():
        o_ref[0] = (acc_sc[...] / l_sc[...]).astype(o_ref.dtype)


def _fox_decode(q, k_new, v_new, lf_new, pool_k, pool_v, pool_lf_t, page_table, pages_per_step):
    db, heads, dh = q.shape
    assert heads == HEAD_ROWS and pool_k.shape[1] == PAGE_SIZE
    n_pages = page_table.shape[1]
    npg = pages_per_step
    steps = n_pages // npg

    def page_map(t, rank):
        return lambda b, j, pt: (pt[b, n_pages - 1 - (j * npg + t)],) + (0,) * (rank - 1)

    row = lambda b, j, pt: (b, 0, 0)
    in_specs = [pl.BlockSpec((1, heads, dh), row), pl.BlockSpec((1, heads, dh), row),
                pl.BlockSpec((1, heads, dh), row), pl.BlockSpec((1, heads, 1), row)]
    in_specs += [pl.BlockSpec((1, PAGE_SIZE, heads, dh), page_map(t, 4)) for t in range(npg)]
    in_specs += [pl.BlockSpec((1, PAGE_SIZE, heads, dh), page_map(t, 4)) for t in range(npg)]
    in_specs += [pl.BlockSpec((1, heads, PAGE_SIZE), page_map(t, 3)) for t in range(npg)]
    page_bytes = _nbytes((PAGE_SIZE, heads, dh), F32)
    blocks = npg * (2 * page_bytes + _nbytes((heads, PAGE_SIZE), F32))
    return pl.pallas_call(
        functools.partial(_fox_decode_body, heads=heads, dh=dh, pages_per_step=npg, scale=dh ** -0.5),
        grid_spec=pltpu.PrefetchScalarGridSpec(
            num_scalar_prefetch=1,
            grid=(db, steps),
            in_specs=in_specs,
            out_specs=pl.BlockSpec((1, heads, dh), row),
            scratch_shapes=[pltpu.VMEM((PAGE_SIZE, PAGE_SIZE * heads), BF16),
                            pltpu.VMEM((heads, 1), F32), pltpu.VMEM((heads, 1), F32),
                            pltpu.VMEM((heads, dh), F32), pltpu.VMEM((heads, 1), F32)]),
        out_shape=jax.ShapeDtypeStruct((db, heads, dh), BF16),
        compiler_params=_params(("parallel", "arbitrary"), blocks, 4 * page_bytes),
        name="fox_decode_attention",
    )(page_table, q, k_new, v_new, lf_new, *([pool_k] * npg), *([pool_v] * npg), *([pool_lf_t] * npg))


def _mem_decode_body(q_ref, k_ref, v_ref, o_ref, m_sc, l_sc, acc_sc, *, heads, dh, scale):
    mask = _head_masks(heads, dh)
    m_sc[...] = jnp.full(m_sc.shape, -jnp.inf, F32)
    l_sc[...] = jnp.zeros(l_sc.shape, F32)
    acc_sc[...] = jnp.zeros(acc_sc.shape, F32)
    qbd = _block_diag(q_ref[0], mask).astype(BF16)
    _decode_block(k_ref[0], v_ref[0], None, qbd, scale, m_sc, l_sc, acc_sc)
    o_ref[0] = _decode_output(mask, l_sc, acc_sc).astype(o_ref.dtype)


def _mem_decode(q, k, v, heads, dh):
    db, n_mem, width = k.shape
    blocks = 2 * _nbytes((n_mem, width), F32) + 2 * _nbytes((1, width), F32)
    return pl.pallas_call(
        functools.partial(_mem_decode_body, heads=heads, dh=dh, scale=dh ** -0.5),
        grid=(db,),
        in_specs=[pl.BlockSpec((1, 1, width), lambda b: (b, 0, 0)),
                  pl.BlockSpec((1, n_mem, width), lambda b: (b, 0, 0)),
                  pl.BlockSpec((1, n_mem, width), lambda b: (b, 0, 0))],
        out_specs=pl.BlockSpec((1, 1, width), lambda b: (b, 0, 0)),
        out_shape=jax.ShapeDtypeStruct((db, 1, width), BF16),
        scratch_shapes=[pltpu.VMEM((HEAD_ROWS, 1), F32), pltpu.VMEM((HEAD_ROWS, 1), F32),
                        pltpu.VMEM((HEAD_ROWS, width), F32)],
        compiler_params=_params(("parallel",), blocks, 4 * _nbytes((n_mem, width), F32)),
        name="memory_decode_attention",
    )(q.reshape(db, 1, width), k, v)


def _head_norm_gate(h, w_row, og):
    hn = h * lax.rsqrt(jnp.mean(h * h, axis=-1, keepdims=True) + EPS)
    return hn * w_row * og.astype(F32)


def _mlstm_chunk_body(q_ref, k_ref, v_ref, og_ref, w_ref, gcol_ref, bcol_ref, irow_ref, brow_ref, wsrc_ref,
                      y_ref, c_ref, n_ref, m_ref, wdst_ref, *, k_scale):
    h = pl.program_id(1)
    ci = pl.program_id(2)
    L = q_ref.shape[0]
    wdst_ref[...] = wsrc_ref[...].astype(wdst_ref.dtype)

    @pl.when(ci == 0)
    def _():
        c_ref[...] = jnp.zeros(c_ref.shape, F32)
        n_ref[...] = jnp.zeros(n_ref.shape, F32)
        m_ref[...] = jnp.zeros(m_ref.shape, F32)

    q = q_ref[...]
    kf = k_ref[...].astype(F32) * k_scale
    k = kf.astype(BF16)
    v = v_ref[...]
    c_state = c_ref[0, 0]
    n_state = n_ref[0, 0]
    m_prev = m_ref[0, 0][:, 0:1]
    b_col = _lane_column(bcol_ref[0], MLF_LANE0 + h)
    i_col = _lane_column(gcol_ref[0], MIG_LANE0 + h)
    b_row = brow_ref[0, 0]
    i_row = irow_ref[0, 0]

    r = lax.broadcasted_iota(jnp.int32, (L, L), 0)
    c = lax.broadcasted_iota(jnp.int32, (L, L), 1)
    d = jnp.where(c <= r, b_col - b_row + i_row, -jnp.inf)
    inter = b_col + m_prev
    m_t = jnp.maximum(inter, jnp.max(d, axis=1, keepdims=True))
    w_intra = jnp.exp(d - m_t)
    w_inter = jnp.exp(inter - m_t)
    a = w_intra * lax.dot_general(q, k, NT_DIMS, preferred_element_type=F32)
    num = (jnp.dot(a.astype(BF16), v, preferred_element_type=F32)
           + w_inter * lax.dot_general(q, c_state.astype(BF16), NT_DIMS, preferred_element_type=F32))
    den = (jnp.sum(a, axis=1, keepdims=True)
           + w_inter * jnp.sum(q.astype(F32) * n_state, axis=1, keepdims=True))
    hcur = num / jnp.maximum(jnp.abs(den), jnp.exp(-m_t))
    y_ref[...] = _head_norm_gate(hcur, w_ref[...], og_ref[...]).astype(y_ref.dtype)

    m_new = m_t[L - 1:L, :]
    b_last = b_col[L - 1:L, :]
    g_inter = jnp.exp(b_last + m_prev - m_new)
    g_intra = jnp.exp(b_last - b_col + i_col - m_new)
    kg = g_intra * kf
    c_ref[0, 0] = g_inter * c_state + lax.dot_general(
        v, kg.astype(BF16), (((0,), (0,)), ((), ())), preferred_element_type=F32)
    n_ref[0, 0] = g_inter * n_state + jnp.sum(kg, axis=0, keepdims=True)
    m_ref[0, 0] = jnp.broadcast_to(m_new, (1, LANES))


def _mlstm_prompt(p1, q_col0, k_col0, v_col0, p2, og_col0, norm_w, g, b_loc, i_row, b_row,
                  batch, seq, heads, dk, dv, k_scale, cast_w, cast_layer):
    L = M_CHUNK
    nc = seq // L
    qo, ko, vo, oo = q_col0 // dk, k_col0 // dk, v_col0 // dv, og_col0 // dv
    w_in_spec, w_out_spec, w_shape, w_bytes = _ride_cast(
        cast_w, cast_layer, batch * heads * nc, lambda b, h, c: (b * heads + h) * nc + c)
    blocks = (2 * _nbytes((L, dk), BF16) + 3 * _nbytes((L, dv), BF16) + 2 * _nbytes((L, LANES), F32)
              + _nbytes((dv, dk), F32) + w_bytes)
    return pl.pallas_call(
        functools.partial(_mlstm_chunk_body, k_scale=k_scale),
        grid=(batch, heads, nc),
        in_specs=[pl.BlockSpec((L, dk), lambda b, h, c: (b * nc + c, qo + h)),
                  pl.BlockSpec((L, dk), lambda b, h, c: (b * nc + c, ko + h)),
                  pl.BlockSpec((L, dv), lambda b, h, c: (b * nc + c, vo + h)),
                  pl.BlockSpec((L, dv), lambda b, h, c: (b * nc + c, oo + h)),
                  pl.BlockSpec((1, dv), lambda b, h, c: (0, h)),
                  pl.BlockSpec((1, L, LANES), lambda b, h, c: (b, c, 0)),
                  pl.BlockSpec((1, L, LANES), lambda b, h, c: (b, c, 0)),
                  pl.BlockSpec((1, 1, 1, L), lambda b, h, c: (b, h, 0, c)),
                  pl.BlockSpec((1, 1, 1, L), lambda b, h, c: (b, h, 0, c)),
                  w_in_spec],
        out_specs=[pl.BlockSpec((L, dv), lambda b, h, c: (b * nc + c, h)),
                   pl.BlockSpec((1, 1, dv, dk), lambda b, h, c: (b, h, 0, 0)),
                   pl.BlockSpec((1, 1, 1, dk), lambda b, h, c: (b, h, 0, 0)),
                   pl.BlockSpec((1, 1, 1, LANES), lambda b, h, c: (b, h, 0, 0)),
                   w_out_spec],
        out_shape=[jax.ShapeDtypeStruct((batch * seq, heads * dv), BF16),
                   jax.ShapeDtypeStruct((batch, heads, dv, dk), F32),
                   jax.ShapeDtypeStruct((batch, heads, 1, dk), F32),
                   jax.ShapeDtypeStruct((batch, heads, 1, LANES), F32),
                   w_shape],
        compiler_params=_params(("parallel", "parallel", "arbitrary"), blocks,
                                8 * _nbytes((L, L), F32) + 4 * _nbytes((L, dv), F32)
                                + 3 * _nbytes((dv, dk), F32)),
        name="mlstm_chunkwise",
    )(p1, p1, p1, p2, norm_w, g, b_loc, i_row, b_row, cast_w)


def _mlstm_step_body(q_ref, k_ref, v_ref, og_ref, w_ref, g_ref, c_ref, n_ref, m_ref,
                     y_ref, co_ref, no_ref, mo_ref, *, heads, dk, dv, k_scale):
    g = g_ref[0]
    eye = (lax.broadcasted_iota(jnp.int32, (dv, dv), 0)
           == lax.broadcasted_iota(jnp.int32, (dv, dv), 1)).astype(BF16)
    lane = lax.broadcasted_iota(jnp.int32, (1, LANES), 1)
    m_out = jnp.zeros((1, LANES), F32)
    for h in range(heads):
        q = q_ref[0, :, h * dk:(h + 1) * dk].astype(F32)
        kf = k_ref[0, :, h * dk:(h + 1) * dk].astype(F32) * k_scale
        v = v_ref[0, :, h * dv:(h + 1) * dv].astype(F32)
        ig = g[:, MIG_LANE0 + h:MIG_LANE0 + h + 1]
        lf = g[:, MLF_LANE0 + h:MLF_LANE0 + h + 1]
        m_prev = m_ref[0][:, h:h + 1]
        c_state = c_ref[0, h]
        n_state = n_ref[0, h]
        inter = lf + m_prev
        m_t = jnp.maximum(inter, ig)
        w_intra = jnp.exp(ig - m_t)
        w_inter = jnp.exp(inter - m_t)
        a = w_intra * jnp.sum(q * kf, axis=1, keepdims=True)
        q_rows = jnp.broadcast_to(q, (HEAD_ROWS, dk)).astype(BF16)
        cq = lax.dot_general(q_rows, c_state.astype(BF16), NT_DIMS, preferred_element_type=F32)[0:1]
        num = a * v + w_inter * cq
        den = a + w_inter * jnp.sum(q * n_state, axis=1, keepdims=True)
        hcur = num / jnp.maximum(jnp.abs(den), jnp.exp(-m_t))
        y_ref[0, :, h * dv:(h + 1) * dv] = _head_norm_gate(
            hcur, w_ref[:, h * dv:(h + 1) * dv], og_ref[0, :, h * dv:(h + 1) * dv]).astype(y_ref.dtype)
        v_rows = jnp.broadcast_to(v, (HEAD_ROWS, dv)).astype(BF16)
        v_col = lax.dot_general(eye, v_rows, NT_DIMS, preferred_element_type=F32)[:, 0:1]
        kg = w_intra * kf
        co_ref[0, h] = w_inter * c_state + v_col * kg
        no_ref[0, h] = w_inter * n_state + kg
        m_out = jnp.where(lane == h, m_t, m_out)
    mo_ref[0] = m_out


def _mlstm_step(q, k, v, og, norm_w, g, c0, n0, m0, heads, dk, dv, k_scale):
    db = q.shape[0]
    row = lambda b: (b, 0, 0)
    st = lambda b: (b, 0, 0, 0)
    blocks = 2 * _nbytes((heads, dv, dk), F32) + 4 * _nbytes((1, heads * dv), F32)
    return pl.pallas_call(
        functools.partial(_mlstm_step_body, heads=heads, dk=dk, dv=dv, k_scale=k_scale),
        grid=(db,),
        in_specs=[pl.BlockSpec((1, 1, heads * dk), row), pl.BlockSpec((1, 1, heads * dk), row),
                  pl.BlockSpec((1, 1, heads * dv), row), pl.BlockSpec((1, 1, heads * dv), row),
                  pl.BlockSpec((1, heads * dv), lambda b: (0, 0)),
                  pl.BlockSpec((1, 1, LANES), row),
                  pl.BlockSpec((1, heads, dv, dk), st), pl.BlockSpec((1, heads, 1, dk), st),
                  pl.BlockSpec((1, 1, heads), row)],
        out_specs=[pl.BlockSpec((1, 1, heads * dv), row),
                   pl.BlockSpec((1, heads, dv, dk), st), pl.BlockSpec((1, heads, 1, dk), st),
                   pl.BlockSpec((1, 1, LANES), row)],
        out_shape=[jax.ShapeDtypeStruct((db, 1, heads * dv), BF16),
                   jax.ShapeDtypeStruct((db, heads, dv, dk), F32),
                   jax.ShapeDtypeStruct((db, heads, 1, dk), F32),
                   jax.ShapeDtypeStruct((db, 1, LANES), F32)],
        compiler_params=_params(("parallel",), blocks, 6 * _nbytes((dv, dk), F32)),
        name="mlstm_step",
    )(q, k, v, og, norm_w, g, c0, n0, m0)


def kernel(x_prompt, x_sample, mem_prompt, cache_fox_k, cache_fox_v, cache_fox_logf, state_mlstm_C, state_mlstm_n, state_mlstm_m, cache_mem_k, cache_mem_v, page_table, attn_norm_w, w_in, fox_f_bias, m_i_bias, m_f_bias, m_norm_w, mem_norm_w, w_mem_kv, w_br_fox, w_br_m, w_br_mem, w_out, ffn_norm_w, w_up, w_down, final_norm_w):
    batch, seq, d_model = x_prompt.shape
    dec_batch, dec_seq, _ = x_sample.shape
    depth = w_in.shape[0]
    assert depth == 1 and dec_seq == 1
    fox_heads, fox_dh = cache_fox_k.shape[3], cache_fox_k.shape[4]
    m_heads, m_dv, m_dk = state_mlstm_C.shape[2:]
    n_mem, mem_heads, mem_dh = cache_mem_k.shape[2:]
    fox_w, mqk_w, mv_w, mem_w = fox_heads * fox_dh, m_heads * m_dk, m_heads * m_dv, mem_heads * mem_dh
    assert fox_heads == MIG_LANE0 and m_heads == MLF_LANE0 - MIG_LANE0
    tp, ts = batch * seq, dec_batch * dec_seq
    k_scale = m_dk ** -0.5

    sizes = (fox_w, fox_w, fox_w, fox_heads, mqk_w, mqk_w, mv_w, m_heads, m_heads, mv_w, mem_w,
             3 * d_model)
    offs = [0]
    for s in sizes:
        offs.append(offs[-1] + s)

    l = 0
    w_in_t = jnp.swapaxes(w_in, 1, 2)
    wa_cols, wb_cols, wc_cols = 3 * fox_w, 2 * mqk_w + mv_w, mv_w + mem_w + 3 * d_model
    w_a = _window_cast(w_in_t, l, offs[0], wa_cols, 512, 1024)
    w_b = _window_cast(w_in_t, l, offs[4], wb_cols, 512, 1024)
    w_c = _window_cast(w_in_t, l, offs[9], wc_cols, 512, 1024)
    PB_MQ, PB_MK, PB_MV = 0, mqk_w, 2 * mqk_w
    PC_OG, PC_QQ, PC_GATES = 0, mv_w, mv_w + mem_w
    n_small = fox_heads + 2 * m_heads
    w_sm = _gate_weights(w_in_t, l, offs[3], offs[7], fox_heads, 2 * m_heads)
    bias_sm = jnp.concatenate([fox_f_bias[l], m_i_bias[l], m_f_bias[l],
                               jnp.zeros((LANES - n_small,), F32)]).reshape(1, LANES).astype(F32)
    w_memkv_b = w_mem_kv[l].astype(BF16)
    w_brf, w_brm, w_brq = w_br_fox[l].astype(BF16), w_br_m[l].astype(BF16), w_br_mem[l].astype(BF16)
    w_out_b = w_out[l].astype(BF16)
    d_ff = w_up.shape[2]
    m_norm = m_norm_w[l].reshape(1, mv_w).astype(F32)

    def wide_tn(*col_counts):
        return 1024 if all(c % 1024 == 0 for c in col_counts) else 512

    def project(x2d, tm):
        h = _rmsnorm(x2d, attn_norm_w[l], BF16, min(tm, 256))
        tn = 512
        fq = _matmul(h, w_a, n_cols=fox_w, col_off=0, out_dtype=BF16, tm=tm, tn=tn)
        fk = _matmul(h, w_a, n_cols=fox_w, col_off=fox_w, out_dtype=F32, tm=tm, tn=tn)
        fv = _matmul(h, w_a, n_cols=fox_w, col_off=2 * fox_w, out_dtype=F32, tm=tm, tn=tn)
        pb = _matmul(h, w_b, n_cols=wb_cols, out_dtype=BF16, tm=tm, tn=wide_tn(wb_cols))
        pc = _matmul(h, w_c, n_cols=wc_cols, out_dtype=BF16, tm=tm, tn=wide_tn(wc_cols, PC_QQ, PC_GATES),
                     epilogue="sigmoid", plain_cols=(PC_QQ, PC_GATES))
        sp = _matmul(h, w_sm, n_cols=LANES, out_dtype=F32, tm=tm, tn=LANES)
        return fq, fk, fv, pb, pc, sp

    def mix_and_mlp(x2d, y_fox, y_m, y_mem, pc, tm):
        merged = _merge(y_fox, y_m, y_mem, w_brf, w_brm, w_brq, pc, PC_GATES, tm, 256)
        x1 = _matmul(merged, w_out_b, n_cols=d_model, out_dtype=F32, tm=tm, tn=512, resid=x2d)
        h2 = _rmsnorm(x1, ffn_norm_w[l], BF16, min(tm, 256))
        up = _matmul(h2, w_up_b, n_cols=d_ff, out_dtype=BF16, tm=tm, tn=wide_tn(d_ff), epilogue="relu2")
        x2 = _matmul(up, w_down_b, n_cols=d_model, out_dtype=F32, tm=min(tm, 512), tn=256, resid=x1)
        return _rmsnorm(x2, final_norm_w, F32, min(tm, 256))

    xp = x_prompt.reshape(tp, d_model)
    fq, fk, fv, pb, pc, sp = project(xp, 1024)
    g, c_glob, b_loc = _gates(sp.reshape(batch, seq, LANES), bias_sm, M_CHUNK)
    ck = c_glob[:, :, FOX_LANE0:FOX_LANE0 + fox_heads].transpose(0, 2, 1).reshape(batch, fox_heads, 1, seq)
    i_row = g[:, :, MIG_LANE0:MIG_LANE0 + m_heads].transpose(0, 2, 1).reshape(batch, m_heads, 1, seq)
    b_row = b_loc[:, :, MLF_LANE0:MLF_LANE0 + m_heads].transpose(0, 2, 1).reshape(batch, m_heads, 1, seq)

    y_fox, w_down_b = _fox_prompt(fq, 0, fk, fv, c_glob, ck, batch, seq, fox_heads, fox_dh, 512, w_down, l)
    y_m, p_c, p_n, p_m, w_up_b = _mlstm_prompt(pb, PB_MQ, PB_MK, PB_MV, pc, PC_OG, m_norm, g, b_loc, i_row,
                                               b_row, batch, seq, m_heads, m_dk, m_dv, k_scale, w_up, l)
    mem_h = _rmsnorm(mem_prompt.reshape(batch * n_mem, d_model), mem_norm_w[l], BF16, 256)
    mem_k = _matmul(mem_h, w_memkv_b, n_cols=mem_w, col_off=0, out_dtype=F32, tm=batch * n_mem, tn=512)
    mem_v = _matmul(mem_h, w_memkv_b, n_cols=mem_w, col_off=mem_w, out_dtype=F32, tm=batch * n_mem, tn=512)
    y_mem = _mem_prompt(pc, PC_QQ, mem_k, mem_v, batch, seq, n_mem, mem_heads, mem_dh, 512)
    y_prompt = mix_and_mlp(xp, y_fox, y_m, y_mem, pc, 1024).reshape(batch, seq, d_model)

    xs = x_sample.reshape(ts, d_model)
    sfq, sfk, sfv, spb, spc, ssp = project(xs, ts)
    sg = _gates_single(ssp, bias_sm)
    s_flf = sg[:, FOX_LANE0:FOX_LANE0 + fox_heads]
    per_head = lambda a: a.reshape(ts, fox_heads, fox_dh)
    sy_fox = _fox_decode(
        per_head(sfq), per_head(sfk), per_head(sfv), s_flf.reshape(ts, fox_heads, 1),
        cache_fox_k[l], cache_fox_v[l], cache_fox_logf[l].transpose(0, 2, 1), page_table, 8).reshape(ts, fox_w)
    row3 = lambda a: a.reshape(ts, 1, a.shape[-1])
    sy_m, s_c, s_n, s_m = _mlstm_step(
        row3(spb[:, PB_MQ:PB_MQ + mqk_w]), row3(spb[:, PB_MK:PB_MK + mqk_w]), row3(spb[:, PB_MV:PB_MV + mv_w]),
        row3(spc[:, PC_OG:PC_OG + mv_w]), m_norm, row3(sg),
        state_mlstm_C[l], state_mlstm_n[l].reshape(ts, m_heads, 1, m_dk),
        state_mlstm_m[l].reshape(ts, 1, m_heads), m_heads, m_dk, m_dv, k_scale)
    sy_mem = _mem_decode(spc[:, PC_QQ:PC_QQ + mem_w], cache_mem_k[l].reshape(ts, n_mem, mem_w),
                         cache_mem_v[l].reshape(ts, n_mem, mem_w), mem_heads, mem_dh).reshape(ts, mem_w)
    y_sample = mix_and_mlp(xs, sy_fox, sy_m.reshape(ts, mv_w), sy_mem, spc, ts).reshape(dec_batch, dec_seq, d_model)

    lead = lambda a, shape: a.reshape((1,) + shape)
    return (
        y_prompt, y_sample,
        lead(fk, (batch, seq, fox_heads, fox_dh)), lead(fv, (batch, seq, fox_heads, fox_dh)),
        lead(g[:, :, FOX_LANE0:FOX_LANE0 + fox_heads], (batch, seq, fox_heads)),
        lead(p_c, (batch, m_heads, m_dv, m_dk)), lead(p_n, (batch, m_heads, m_dk)),
        lead(p_m[:, :, 0, 0], (batch, m_heads)),
        lead(mem_k, (batch, n_mem, mem_heads, mem_dh)), lead(mem_v, (batch, n_mem, mem_heads, mem_dh)),
        lead(sfk, (dec_batch, dec_seq, fox_heads, fox_dh)), lead(sfv, (dec_batch, dec_seq, fox_heads, fox_dh)),
        lead(s_flf, (dec_batch, dec_seq, fox_heads)),
        lead(s_c, (dec_batch, m_heads, m_dv, m_dk)), lead(s_n, (dec_batch, m_heads, m_dk)),
        lead(s_m[:, 0, :m_heads], (dec_batch, m_heads)),
    )
```

```python
import functools

import jax
import jax.numpy as jnp
from jax import lax
from jax.experimental import pallas as pl
from jax.experimental.pallas import tpu as pltpu

F32 = jnp.float32
BF16 = jnp.bfloat16
HIGHEST = lax.Precision.HIGHEST

EPS = 1e-6
PAGE_SIZE = 128
LANES = 128
VMEM_LIMIT_CAP = 60000 * 1024
VMEM_SLACK = 6 * 1024 * 1024

FOX_LANE0 = 0
MIG_LANE0 = 16
MLF_LANE0 = 24
M_CHUNK = 256
NT_DIMS = (((1,), (1,)), ((), ()))


def _nbytes(shape, dtype):
    n = 1
    for s in shape:
        n *= s
    return n * jnp.dtype(dtype).itemsize


def _params(semantics, block_bytes, extra_bytes=0):
    limit = min(2 * block_bytes + extra_bytes + VMEM_SLACK, VMEM_LIMIT_CAP)
    return pltpu.CompilerParams(dimension_semantics=semantics, vmem_limit_bytes=int(limit))


def _log_sigmoid(x):
    return jnp.minimum(x, 0.0) - jnp.log1p(jnp.exp(-jnp.abs(x)))


def _sigmoid(x):
    return 1.0 / (1.0 + jnp.exp(-x))


def _rmsnorm_body(x_ref, w_ref, o_ref):
    x = x_ref[...].astype(F32)
    y = x * lax.rsqrt(jnp.mean(x * x, axis=-1, keepdims=True) + EPS)
    o_ref[...] = (y * w_ref[...]).astype(o_ref.dtype)


def _rmsnorm(x, w, out_dtype, tm):
    m, d = x.shape
    blocks = _nbytes((tm, d), x.dtype) + _nbytes((tm, d), out_dtype) + _nbytes((1, d), F32)
    return pl.pallas_call(
        _rmsnorm_body,
        grid=(m // tm,),
        in_specs=[pl.BlockSpec((tm, d), lambda i: (i, 0)),
                  pl.BlockSpec((1, d), lambda i: (0, 0))],
        out_specs=pl.BlockSpec((tm, d), lambda i: (i, 0)),
        out_shape=jax.ShapeDtypeStruct((m, d), out_dtype),
        compiler_params=_params(("parallel",), blocks, _nbytes((tm, d), F32) * 2),
        name="rmsnorm",
    )(x, w.reshape(1, d).astype(F32))


TAIL_ROWS = 32


def _window_cast_body(*refs, shift):
    a_ref, o_ref = refs[0], refs[-1]
    a = a_ref[...]
    if shift:
        wide = jnp.concatenate([a, refs[1][...]], axis=0)
        a = wide[shift:shift + a.shape[0], :]
    o_ref[...] = a.T.astype(o_ref.dtype)


def _window_cast(wt, layer, row0, n_rows, tn, tk):
    kdim = wt.shape[2]
    base = row0 // tn * tn
    shift = row0 - base
    assert kdim % tk == 0 and n_rows % tn == 0 and tn % TAIL_ROWS == 0
    assert shift % 8 == 0 and shift <= TAIL_ROWS and row0 + n_rows <= wt.shape[1]
    jb = base // tn
    in_specs = [pl.BlockSpec((None, tn, tk), lambda j, i: (layer, jb + j, i))]
    args = [wt]
    if shift:
        in_specs.append(pl.BlockSpec((None, TAIL_ROWS, tk),
                                     lambda j, i: (layer, (jb + j + 1) * (tn // TAIL_ROWS), i)))
        args.append(wt)
    blocks = _nbytes((tn + TAIL_ROWS, tk), F32) + _nbytes((tk, tn), BF16)
    return pl.pallas_call(
        functools.partial(_window_cast_body, shift=shift),
        grid=(n_rows // tn, kdim // tk),
        in_specs=in_specs,
        out_specs=pl.BlockSpec((tk, tn), lambda j, i: (i, j)),
        out_shape=jax.ShapeDtypeStruct((kdim, n_rows), BF16),
        compiler_params=_params(("parallel", "parallel"), blocks, 3 * _nbytes((tn + TAIL_ROWS, tk), F32)),
        name="window_cast",
    )(*args)


def _gate_weight_body(f_ref, m_ref, o_ref):
    rows = jnp.concatenate(
        [f_ref[...], m_ref[...],
         jnp.zeros((LANES - f_ref.shape[0] - m_ref.shape[0], f_ref.shape[1]), F32)], axis=0)
    o_ref[...] = rows.T.astype(o_ref.dtype)


def _gate_weights(wt, layer, row_fox, row_m, n_fox, n_m):
    assert row_fox % n_fox == 0 and row_m % n_m == 0 and n_fox % 8 == 0 and n_m % 8 == 0
    kdim = wt.shape[2]
    return pl.pallas_call(
        _gate_weight_body,
        grid=(1,),
        in_specs=[pl.BlockSpec((None, n_fox, kdim), lambda i: (layer, row_fox // n_fox, 0)),
                  pl.BlockSpec((None, n_m, kdim), lambda i: (layer, row_m // n_m, 0))],
        out_specs=pl.BlockSpec((kdim, LANES), lambda i: (0, 0)),
        out_shape=jax.ShapeDtypeStruct((kdim, LANES), BF16),
        name="gate_weights",
    )(wt, wt)


def _ride_cast(w, layer, n_steps, step_of):
    rows, cols = w.shape[1:]
    n_slabs = max(d for d in range(1, n_steps + 1) if rows % d == 0 and (rows // d) % 16 == 0)
    slab = rows // n_slabs
    slab_of = lambda *g: jnp.minimum(step_of(*g), n_slabs - 1)
    in_spec = pl.BlockSpec((None, slab, cols), lambda *g: (layer, slab_of(*g), 0))
    out_spec = pl.BlockSpec((slab, cols), lambda *g: (slab_of(*g), 0))
    block_bytes = _nbytes((slab, cols), F32) + _nbytes((slab, cols), BF16)
    return in_spec, out_spec, jax.ShapeDtypeStruct((rows, cols), BF16), block_bytes


def _mm_body(*refs, nk, epilogue, plain_blocks, has_resid, n_ride):
    x_ref, w_ref = refs[0], refs[1]
    r_ref = refs[2] if has_resid else None
    ride_src = refs[2 + has_resid:2 + has_resid + n_ride]
    o_ref = refs[2 + has_resid + n_ride]
    ride_dst = refs[3 + has_resid + n_ride:3 + has_resid + 2 * n_ride]
    acc_ref = refs[3 + has_resid + 2 * n_ride] if nk > 1 else None
    j = pl.program_id(1)
    for src, dst in zip(ride_src, ride_dst):
        dst[...] = src[...].astype(dst.dtype)
    part = jnp.dot(x_ref[...], w_ref[...], preferred_element_type=F32)

    def finish(acc):
        if epilogue == "sigmoid":
            gated = _sigmoid(acc)
            if plain_blocks is not None:
                gated = jnp.where((j >= plain_blocks[0]) & (j < plain_blocks[1]), acc, gated)
            acc = gated
        elif epilogue == "relu2":
            r = jnp.maximum(acc, 0.0)
            acc = r * r
        if has_resid:
            acc = r_ref[...] + acc
        o_ref[...] = acc.astype(o_ref.dtype)

    if nk == 1:
        finish(part)
    else:
        k = pl.program_id(2)

        @pl.when(k == 0)
        def _():
            acc_ref[...] = part

        @pl.when(k > 0)
        def _():
            acc_ref[...] += part

        @pl.when(k == nk - 1)
        def _():
            finish(acc_ref[...])


def _matmul(x, w, *, n_cols, col_off=0, out_dtype, tm, tn, tk=None, epilogue=None, plain_cols=None,
            resid=None, ride=()):
    m, kdim = x.shape
    tk = kdim if tk is None else tk
    nk = kdim // tk
    assert m % tm == 0 and n_cols % tn == 0 and col_off % tn == 0 and kdim % tk == 0
    joff = col_off // tn
    plain_blocks = None
    if plain_cols is not None:
        assert epilogue == "sigmoid" and plain_cols[0] % tn == 0 and plain_cols[1] % tn == 0
        plain_blocks = (plain_cols[0] // tn, plain_cols[1] // tn)
    in_specs = [pl.BlockSpec((tm, tk), lambda i, j, k: (i, k)),
                pl.BlockSpec((tk, tn), lambda i, j, k: (k, j + joff))]
    args = [x, w]
    blocks = _nbytes((tm, tk), x.dtype) + _nbytes((tk, tn), w.dtype) + _nbytes((tm, tn), out_dtype)
    if resid is not None:
        in_specs.append(pl.BlockSpec((tm, tn), lambda i, j, k: (i, j)))
        args.append(resid)
        blocks += _nbytes((tm, tn), resid.dtype)
    out_specs = [pl.BlockSpec((tm, tn), lambda i, j, k: (i, j))]
    out_shapes = [jax.ShapeDtypeStruct((m, n_cols), out_dtype)]
    nj = n_cols // tn
    for w3d, layer in ride:
        assert nk == 1
        r_in, r_out, r_shape, r_bytes = _ride_cast(w3d, layer, (m // tm) * nj, lambda i, j, k: i * nj + j)
        in_specs.append(r_in)
        args.append(w3d)
        out_specs.append(r_out)
        out_shapes.append(r_shape)
        blocks += r_bytes
    scratch = [pltpu.VMEM((tm, tn), F32)] if nk > 1 else []
    extra = _nbytes((tm, tn), F32) * (3 if nk > 1 else 2)
    outs = pl.pallas_call(
        functools.partial(_mm_body, nk=nk, epilogue=epilogue, plain_blocks=plain_blocks,
                          has_resid=resid is not None, n_ride=len(ride)),
        grid=(m // tm, nj, nk),
        in_specs=in_specs,
        out_specs=out_specs,
        out_shape=out_shapes,
        scratch_shapes=scratch,
        compiler_params=_params(("arbitrary" if ride else "parallel",) * 2 + ("arbitrary",), blocks, extra),
        name="matmul_" + (epilogue or "plain"),
    )(*args)
    return tuple(outs) if ride else outs[0]


def _merge_body(yf_ref, ym_ref, yq_ref, wf_ref, wm_ref, wq_ref, g0_ref, g1_ref, g2_ref, o_ref):
    acc = g0_ref[...].astype(F32) * jnp.dot(yf_ref[...], wf_ref[...], preferred_element_type=F32)
    acc += g1_ref[...].astype(F32) * jnp.dot(ym_ref[...], wm_ref[...], preferred_element_type=F32)
    acc += g2_ref[...].astype(F32) * jnp.dot(yq_ref[...], wq_ref[...], preferred_element_type=F32)
    o_ref[...] = acc.astype(o_ref.dtype)


def _merge(y_fox, y_m, y_mem, w_fox, w_m, w_mem, gates, gate_off, tm, tn):
    t = y_fox.shape[0]
    d = w_fox.shape[1]
    goff = gate_off // tn
    nd = d // tn
    kf, km, kq = y_fox.shape[1], y_m.shape[1], y_mem.shape[1]
    blocks = (_nbytes((tm, kf + km + kq), BF16) + _nbytes((kf + km + kq, tn), BF16)
              + 4 * _nbytes((tm, tn), BF16))
    return pl.pallas_call(
        _merge_body,
        grid=(t // tm, nd),
        in_specs=[pl.BlockSpec((tm, kf), lambda i, j: (i, 0)),
                  pl.BlockSpec((tm, km), lambda i, j: (i, 0)),
                  pl.BlockSpec((tm, kq), lambda i, j: (i, 0)),
                  pl.BlockSpec((kf, tn), lambda i, j: (0, j)),
                  pl.BlockSpec((km, tn), lambda i, j: (0, j)),
                  pl.BlockSpec((kq, tn), lambda i, j: (0, j)),
                  pl.BlockSpec((tm, tn), lambda i, j: (i, goff + j)),
                  pl.BlockSpec((tm, tn), lambda i, j: (i, goff + nd + j)),
                  pl.BlockSpec((tm, tn), lambda i, j: (i, goff + 2 * nd + j))],
        out_specs=pl.BlockSpec((tm, tn), lambda i, j: (i, j)),
        out_shape=jax.ShapeDtypeStruct((t, d), BF16),
        compiler_params=_params(("parallel", "parallel"), blocks, 4 * _nbytes((tm, tn), F32)),
        name="gated_merge",
    )(y_fox, y_m, y_mem, w_fox, w_m, w_mem, gates, gates, gates)


def _gate_values(x):
    lane = lax.broadcasted_iota(jnp.int32, (1, LANES), 1)
    is_ig = (lane >= MIG_LANE0) & (lane < MLF_LANE0)
    return jnp.where(is_ig, x, _log_sigmoid(x))


def _gate_only_body(sp_ref, bias_ref, g_ref):
    g_ref[...] = _gate_values(sp_ref[...] + bias_ref[...])


def _gates_single(sp, bias):
    t = sp.shape[0]
    spec = pl.BlockSpec((t, LANES), lambda i: (0, 0))
    return pl.pallas_call(
        _gate_only_body,
        grid=(1,),
        in_specs=[spec, pl.BlockSpec((1, LANES), lambda i: (0, 0))],
        out_specs=spec,
        out_shape=jax.ShapeDtypeStruct((t, LANES), F32),
        name="small_gates_single",
    )(sp, bias)


def _gate_body(sp_ref, bias_ref, g_ref, cg_ref, bl_ref, *, seq, chunk):
    rows = chunk
    r = lax.broadcasted_iota(jnp.int32, (rows, rows), 0)
    c = lax.broadcasted_iota(jnp.int32, (rows, rows), 1)
    tri = (r >= c).astype(F32)
    carry = jnp.zeros((1, LANES), F32)
    for t in range(seq // rows):
        sl = slice(t * rows, (t + 1) * rows)
        g = _gate_values(sp_ref[0, sl, :] + bias_ref[...])
        g_ref[0, sl, :] = g
        loc = jnp.dot(tri, g, precision=HIGHEST, preferred_element_type=F32)
        bl_ref[0, sl, :] = loc
        cg_ref[0, sl, :] = loc + carry
        carry = carry + loc[rows - 1:rows, :]


def _gates(sp, bias, chunk):
    b, s, _ = sp.shape
    spec = pl.BlockSpec((1, s, LANES), lambda i: (i, 0, 0))
    shape = jax.ShapeDtypeStruct((b, s, LANES), F32)
    return pl.pallas_call(
        functools.partial(_gate_body, seq=s, chunk=chunk),
        grid=(b,),
        in_specs=[spec, pl.BlockSpec((1, LANES), lambda i: (0, 0))],
        out_specs=[spec, spec, spec],
        out_shape=[shape, shape, shape],
        compiler_params=_params(("parallel",), 4 * _nbytes((s, LANES), F32), _nbytes((s, LANES), F32)),
        name="small_gates",
    )(sp, bias)


def _lane_column(tile, lane_index):
    lane = lax.broadcasted_iota(jnp.int32, (1, LANES), 1)
    return jnp.sum(jnp.where(lane == lane_index, tile, 0.0), axis=1, keepdims=True)


def _fox_prompt_body(q_ref, k_ref, v_ref, cq_ref, ck_ref, wsrc_ref, o_ref, wdst_ref, kb_sc, vb_sc, *,
                     scale, tq):
    h = pl.program_id(1)
    seq = q_ref.shape[0]
    wdst_ref[...] = wsrc_ref[...].astype(wdst_ref.dtype)
    kb_sc[...] = k_ref[...].astype(BF16)
    vb_sc[...] = v_ref[...].astype(BF16)
    r = lax.broadcasted_iota(jnp.int32, (tq, tq), 0)
    c = lax.broadcasted_iota(jnp.int32, (tq, tq), 1)
    causal = c <= r
    for i in range(seq // tq):
        lo, hi = i * tq, (i + 1) * tq
        q = q_ref[lo:hi, :]
        cq = _lane_column(cq_ref[0, lo:hi, :], h)

        def scores(a, b):
            s = lax.dot_general(q, kb_sc[a:b, :], NT_DIMS, preferred_element_type=F32) * scale
            return s + cq - ck_ref[0, 0, :, a:b]

        s_d = jnp.where(causal, scores(lo, hi), -jnp.inf)
        m = jnp.max(s_d, axis=1, keepdims=True)
        if i > 0:
            s_o = scores(0, lo)
            m = jnp.maximum(m, jnp.max(s_o, axis=1, keepdims=True))
        p_d = jnp.exp(s_d - m)
        l = jnp.sum(p_d, axis=1, keepdims=True)
        acc = jnp.dot(p_d.astype(BF16), vb_sc[lo:hi, :], preferred_element_type=F32)
        if i > 0:
            p_o = jnp.exp(s_o - m)
            l = l + jnp.sum(p_o, axis=1, keepdims=True)
            acc = acc + jnp.dot(p_o.astype(BF16), vb_sc[0:lo, :], preferred_element_type=F32)
        o_ref[lo:hi, :] = (acc / l).astype(o_ref.dtype)


def _fox_prompt(qsrc, q_col0, k, v, cq, ck, batch, seq, heads, dh, tq, cast_w, cast_layer):
    qoff = q_col0 // dh
    w_in_spec, w_out_spec, w_shape, w_bytes = _ride_cast(
        cast_w, cast_layer, batch * heads, lambda b, h: b * heads + h)
    blocks = (2 * _nbytes((seq, dh), BF16) + 2 * _nbytes((seq, dh), F32)
              + _nbytes((seq, LANES), F32) + _nbytes((8, seq), F32) + w_bytes)
    return pl.pallas_call(
        functools.partial(_fox_prompt_body, scale=dh ** -0.5, tq=tq),
        grid=(batch, heads),
        in_specs=[pl.BlockSpec((seq, dh), lambda b, h: (b, qoff + h)),
                  pl.BlockSpec((seq, dh), lambda b, h: (b, h)),
                  pl.BlockSpec((seq, dh), lambda b, h: (b, h)),
                  pl.BlockSpec((1, seq, LANES), lambda b, h: (b, 0, 0)),
                  pl.BlockSpec((1, 1, 1, seq), lambda b, h: (b, h, 0, 0)),
                  w_in_spec],
        out_specs=[pl.BlockSpec((seq, dh), lambda b, h: (b, h)), w_out_spec],
        out_shape=[jax.ShapeDtypeStruct((batch * seq, heads * dh), BF16), w_shape],
        scratch_shapes=[pltpu.VMEM((seq, dh), BF16), pltpu.VMEM((seq, dh), BF16)],
        compiler_params=_params(("parallel", "parallel"), blocks, 8 * _nbytes((tq, seq), F32)),
        name="fox_prompt_attention",
    )(qsrc, k, v, cq, ck, cast_w)


def _mem_prompt_body(q_ref, k_ref, v_ref, o_ref, *, scale):
    s = lax.dot_general(q_ref[...], k_ref[...].astype(BF16), NT_DIMS,
                        preferred_element_type=F32) * scale
    m = jnp.max(s, axis=1, keepdims=True)
    p = jnp.exp(s - m)
    p = p / jnp.sum(p, axis=1, keepdims=True)
    o_ref[...] = jnp.dot(p.astype(BF16), v_ref[...].astype(BF16),
                         preferred_element_type=F32).astype(o_ref.dtype)


def _mem_prompt(qsrc, q_col0, k, v, batch, seq, n_mem, heads, dh, tq):
    nq = seq // tq
    qoff = q_col0 // dh
    blocks = 2 * _nbytes((tq, dh), BF16) + 2 * _nbytes((n_mem, dh), F32)
    return pl.pallas_call(
        functools.partial(_mem_prompt_body, scale=dh ** -0.5),
        grid=(batch, heads, nq),
        in_specs=[pl.BlockSpec((tq, dh), lambda b, h, i: (b * nq + i, qoff + h)),
                  pl.BlockSpec((n_mem, dh), lambda b, h, i: (b, h)),
                  pl.BlockSpec((n_mem, dh), lambda b, h, i: (b, h))],
        out_specs=pl.BlockSpec((tq, dh), lambda b, h, i: (b * nq + i, h)),
        out_shape=jax.ShapeDtypeStruct((batch * seq, heads * dh), BF16),
        compiler_params=_params(("parallel", "parallel", "parallel"), blocks,
                                4 * _nbytes((tq, n_mem), F32) + _nbytes((tq, dh), F32)),
        name="memory_prompt_attention",
    )(qsrc, k, v)


HEAD_ROWS = 16


def _head_masks(heads, dh):
    assert heads <= HEAD_ROWS
    row = lax.broadcasted_iota(jnp.int32, (HEAD_ROWS, heads * dh), 0)
    col = lax.broadcasted_iota(jnp.int32, (HEAD_ROWS, heads * dh), 1)
    return (col >= row * dh) & (col < (row + 1) * dh)


def _block_diag(row, mask):
    return jnp.where(mask, jnp.broadcast_to(row.astype(F32), mask.shape), 0.0)


def _decode_block(k_blk, v_blk, bias, qbd, scale, m_sc, l_sc, acc_sc):
    s = lax.dot_general(qbd, k_blk.astype(BF16), NT_DIMS, preferred_element_type=F32) * scale
    if bias is not None:
        s = s + bias
    m_prev = m_sc[...]
    m_new = jnp.maximum(m_prev, jnp.max(s, axis=1, keepdims=True))
    alpha = jnp.exp(m_prev - m_new)
    p = jnp.exp(s - m_new)
    l_sc[...] = alpha * l_sc[...] + jnp.sum(p, axis=1, keepdims=True)
    acc_sc[...] = alpha * acc_sc[...] + jnp.dot(p.astype(BF16), v_blk.astype(BF16),
                                               preferred_element_type=F32)
    m_sc[...] = m_new


def _decode_output(mask, l_sc, acc_sc):
    return jnp.sum(jnp.where(mask, acc_sc[...] / l_sc[...], 0.0), axis=0, keepdims=True)


def _split_bf16(x):
    hi = x.astype(BF16)
    r1 = x - hi.astype(F32)
    mid = r1.astype(BF16)
    lo = (r1 - mid.astype(F32)).astype(BF16)
    return hi, mid, lo


def _fox_decode_body(pt_ref, q_ref, kn_ref, vn_ref, lfn_ref, *refs, heads, dh, pages_per_step, scale):
    npg = pages_per_step
    k_refs = refs[0:npg]
    v_refs = refs[npg:2 * npg]
    lf_refs = refs[2 * npg:3 * npg]
    o_ref = refs[3 * npg]
    spread_sc, m_sc, l_sc, acc_sc, carry_sc = refs[3 * npg + 1:]
    j = pl.program_id(1)
    flat = PAGE_SIZE * heads

    @pl.when(j == 0)
    def _():
        m_sc[...] = jnp.sum(q_ref[0].astype(F32) * kn_ref[0], axis=1, keepdims=True) * scale
        l_sc[...] = jnp.ones(l_sc.shape, F32)
        acc_sc[...] = vn_ref[0]
        carry_sc[...] = lfn_ref[0]
        key_of = lax.broadcasted_iota(jnp.int32, (PAGE_SIZE, flat), 1) // heads
        spread_sc[...] = (key_of == lax.broadcasted_iota(jnp.int32, (PAGE_SIZE, flat), 0)).astype(BF16)

    r = lax.broadcasted_iota(jnp.int32, (PAGE_SIZE, PAGE_SIZE), 0)
    c = lax.broadcasted_iota(jnp.int32, (PAGE_SIZE, PAGE_SIZE), 1)
    later = (r > c).astype(F32)
    carry = carry_sc[...]
    pieces = []
    for t in range(npg):
        lf = lf_refs[t][0]
        pieces.extend(_split_bf16(carry + jnp.dot(lf, later, precision=HIGHEST,
                                                  preferred_element_type=F32)))
        carry = carry + jnp.sum(lf, axis=1, keepdims=True)
    carry_sc[...] = carry
    bias_flat = jnp.dot(jnp.concatenate(pieces, axis=0), spread_sc[...], preferred_element_type=F32)

    own_head = (lax.broadcasted_iota(jnp.int32, (heads, flat), 1) % heads
                == lax.broadcasted_iota(jnp.int32, (heads, flat), 0))
    q = q_ref[0]
    scores = []
    for t in range(npg):
        b0 = 3 * heads * t
        bias = (bias_flat[b0:b0 + heads] + bias_flat[b0 + heads:b0 + 2 * heads]
                + bias_flat[b0 + 2 * heads:b0 + 3 * heads])
        k2 = k_refs[t][0].reshape(flat, dh).astype(BF16)
        s = lax.dot_general(q, k2, NT_DIMS, preferred_element_type=F32) * scale + bias
        scores.append(jnp.where(own_head, s, -jnp.inf))
    m_prev = m_sc[...]
    m_new = m_prev
    for s in scores:
        m_new = jnp.maximum(m_new, jnp.max(s, axis=1, keepdims=True))
    alpha = jnp.exp(m_prev - m_new)
    l_new = alpha * l_sc[...]
    acc = alpha * acc_sc[...]
    for t in range(npg):
        p = jnp.exp(scores[t] - m_new)
        l_new = l_new + jnp.sum(p, axis=1, keepdims=True)
        v2 = v_refs[t][0].reshape(flat, dh).astype(BF16)
        acc = acc + jnp.dot(p.astype(BF16), v2, preferred_element_type=F32)
    l_sc[...] = l_new
    acc_sc[...] = acc
    m_sc[...] = m_new

    @pl.when(j == pl.num_programs(1) - 1)
    def _():
        o_ref[0] = (acc_sc[...] / l_sc[...]).astype(o_ref.dtype)


def _fox_decode(q, k_new, v_new, lf_new, pool_k, pool_v, pool_lf_t, page_table, pages_per_step):
    db, heads, dh = q.shape
    assert heads == HEAD_ROWS and pool_k.shape[1] == PAGE_SIZE
    n_pages = page_table.shape[1]
    npg = pages_per_step
    steps = n_pages // npg

    def page_map(t, rank):
        return lambda b, j, pt: (pt[b, n_pages - 1 - (j * npg + t)],) + (0,) * (rank - 1)

    row = lambda b, j, pt: (b, 0, 0)
    in_specs = [pl.BlockSpec((1, heads, dh), row), pl.BlockSpec((1, heads, dh), row),
                pl.BlockSpec((1, heads, dh), row), pl.BlockSpec((1, heads, 1), row)]
    in_specs += [pl.BlockSpec((1, PAGE_SIZE, heads, dh), page_map(t, 4)) for t in range(npg)]
    in_specs += [pl.BlockSpec((1, PAGE_SIZE, heads, dh), page_map(t, 4)) for t in range(npg)]
    in_specs += [pl.BlockSpec((1, heads, PAGE_SIZE), page_map(t, 3)) for t in range(npg)]
    page_bytes = _nbytes((PAGE_SIZE, heads, dh), F32)
    blocks = npg * (2 * page_bytes + _nbytes((heads, PAGE_SIZE), F32))
    return pl.pallas_call(
        functools.partial(_fox_decode_body, heads=heads, dh=dh, pages_per_step=npg, scale=dh ** -0.5),
        grid_spec=pltpu.PrefetchScalarGridSpec(
            num_scalar_prefetch=1,
            grid=(db, steps),
            in_specs=in_specs,
            out_specs=pl.BlockSpec((1, heads, dh), row),
            scratch_shapes=[pltpu.VMEM((PAGE_SIZE, PAGE_SIZE * heads), BF16),
                            pltpu.VMEM((heads, 1), F32), pltpu.VMEM((heads, 1), F32),
                            pltpu.VMEM((heads, dh), F32), pltpu.VMEM((heads, 1), F32)]),
        out_shape=jax.ShapeDtypeStruct((db, heads, dh), BF16),
        compiler_params=_params(("parallel", "arbitrary"), blocks, 4 * page_bytes),
        name="fox_decode_attention",
    )(page_table, q, k_new, v_new, lf_new, *([pool_k] * npg), *([pool_v] * npg), *([pool_lf_t] * npg))


def _mem_decode_body(q_ref, k_ref, v_ref, o_ref, *, heads, scale):
    n_mem, _, dh = k_ref.shape
    flat = n_mem * heads
    q = jnp.concatenate([q_ref[0], jnp.zeros((HEAD_ROWS - heads, dh), q_ref.dtype)], axis=0)
    k2 = k_ref[...].reshape(flat, dh).astype(BF16)
    v2 = v_ref[...].reshape(flat, dh).astype(BF16)
    own_head = (lax.broadcasted_iota(jnp.int32, (HEAD_ROWS, flat), 1) % heads
                == lax.broadcasted_iota(jnp.int32, (HEAD_ROWS, flat), 0) % heads)
    s = lax.dot_general(q, k2, NT_DIMS, preferred_element_type=F32) * scale
    s = jnp.where(own_head, s, -jnp.inf)
    m = jnp.max(s, axis=1, keepdims=True)
    p = jnp.exp(s - m)
    l = jnp.sum(p, axis=1, keepdims=True)
    o = jnp.dot(p.astype(BF16), v2, preferred_element_type=F32)
    o_ref[0] = (o[0:heads] / l[0:heads]).astype(o_ref.dtype)


def _mem_decode(q, k, v, layer):
    db, heads, dh = q.shape
    n_mem = k.shape[2]
    assert heads <= HEAD_ROWS
    cache_spec = pl.BlockSpec((None, None, n_mem, heads, dh), lambda b: (layer, b, 0, 0, 0))
    blocks = 2 * _nbytes((n_mem, 8, dh), F32) + 2 * _nbytes((8, dh), F32)
    return pl.pallas_call(
        functools.partial(_mem_decode_body, heads=heads, scale=dh ** -0.5),
        grid=(db,),
        in_specs=[pl.BlockSpec((1, heads, dh), lambda b: (b, 0, 0)), cache_spec, cache_spec],
        out_specs=pl.BlockSpec((1, heads, dh), lambda b: (b, 0, 0)),
        out_shape=jax.ShapeDtypeStruct((db, heads, dh), BF16),
        compiler_params=_params(("parallel",), blocks, 6 * _nbytes((n_mem * heads, dh), F32)),
        name="memory_decode_attention",
    )(q, k, v)


def _head_norm_gate(h, w_row, og):
    hn = h * lax.rsqrt(jnp.mean(h * h, axis=-1, keepdims=True) + EPS)
    return hn * w_row * og.astype(F32)


def _mlstm_chunk_body(q_ref, k_ref, v_ref, og_ref, w_ref, gcol_ref, bcol_ref, irow_ref, brow_ref, wsrc_ref,
                      y_ref, c_ref, n_ref, m_ref, wdst_ref, *, k_scale):
    h = pl.program_id(1)
    ci = pl.program_id(2)
    L = q_ref.shape[0]
    wdst_ref[...] = wsrc_ref[...].astype(wdst_ref.dtype)

    @pl.when(ci == 0)
    def _():
        c_ref[...] = jnp.zeros(c_ref.shape, F32)
        n_ref[...] = jnp.zeros(n_ref.shape, F32)
        m_ref[...] = jnp.zeros(m_ref.shape, F32)

    q = q_ref[...]
    kf = k_ref[...].astype(F32) * k_scale
    k = kf.astype(BF16)
    v = v_ref[...]
    c_state = c_ref[0, 0]
    n_state = n_ref[0, 0]
    m_prev = m_ref[0, 0][:, 0:1]
    b_col = _lane_column(bcol_ref[0], MLF_LANE0 + h)
    i_col = _lane_column(gcol_ref[0], MIG_LANE0 + h)
    b_row = brow_ref[0, 0]
    i_row = irow_ref[0, 0]

    r = lax.broadcasted_iota(jnp.int32, (L, L), 0)
    c = lax.broadcasted_iota(jnp.int32, (L, L), 1)
    d = jnp.where(c <= r, b_col - b_row + i_row, -jnp.inf)
    inter = b_col + m_prev
    m_t = jnp.maximum(inter, jnp.max(d, axis=1, keepdims=True))
    w_intra = jnp.exp(d - m_t)
    w_inter = jnp.exp(inter - m_t)
    a = w_intra * lax.dot_general(q, k, NT_DIMS, preferred_element_type=F32)
    num = (jnp.dot(a.astype(BF16), v, preferred_element_type=F32)
           + w_inter * lax.dot_general(q, c_state.astype(BF16), NT_DIMS, preferred_element_type=F32))
    den = (jnp.sum(a, axis=1, keepdims=True)
           + w_inter * jnp.sum(q.astype(F32) * n_state, axis=1, keepdims=True))
    hcur = num / jnp.maximum(jnp.abs(den), jnp.exp(-m_t))
    y_ref[...] = _head_norm_gate(hcur, w_ref[...], og_ref[...]).astype(y_ref.dtype)

    m_new = m_t[L - 1:L, :]
    b_last = b_col[L - 1:L, :]
    g_inter = jnp.exp(b_last + m_prev - m_new)
    g_intra = jnp.exp(b_last - b_col + i_col - m_new)
    kg = g_intra * kf
    c_ref[0, 0] = g_inter * c_state + lax.dot_general(
        v, kg.astype(BF16), (((0,), (0,)), ((), ())), preferred_element_type=F32)
    n_ref[0, 0] = g_inter * n_state + jnp.sum(kg, axis=0, keepdims=True)
    m_ref[0, 0] = jnp.broadcast_to(m_new, (1, LANES))


def _mlstm_prompt(p1, q_col0, k_col0, v_col0, p2, og_col0, norm_w, g, b_loc, i_row, b_row,
                  batch, seq, heads, dk, dv, k_scale, cast_w, cast_layer):
    L = M_CHUNK
    nc = seq // L
    qo, ko, vo, oo = q_col0 // dk, k_col0 // dk, v_col0 // dv, og_col0 // dv
    w_in_spec, w_out_spec, w_shape, w_bytes = _ride_cast(
        cast_w, cast_layer, batch * heads * nc, lambda b, h, c: (b * heads + h) * nc + c)
    blocks = (2 * _nbytes((L, dk), BF16) + 3 * _nbytes((L, dv), BF16) + 2 * _nbytes((L, LANES), F32)
              + _nbytes((dv, dk), F32) + w_bytes)
    return pl.pallas_call(
        functools.partial(_mlstm_chunk_body, k_scale=k_scale),
        grid=(batch, heads, nc),
        in_specs=[pl.BlockSpec((L, dk), lambda b, h, c: (b * nc + c, qo + h)),
                  pl.BlockSpec((L, dk), lambda b, h, c: (b * nc + c, ko + h)),
                  pl.BlockSpec((L, dv), lambda b, h, c: (b * nc + c, vo + h)),
                  pl.BlockSpec((L, dv), lambda b, h, c: (b * nc + c, oo + h)),
                  pl.BlockSpec((1, dv), lambda b, h, c: (0, h)),
                  pl.BlockSpec((1, L, LANES), lambda b, h, c: (b, c, 0)),
                  pl.BlockSpec((1, L, LANES), lambda b, h, c: (b, c, 0)),
                  pl.BlockSpec((1, 1, 1, L), lambda b, h, c: (b, h, 0, c)),
                  pl.BlockSpec((1, 1, 1, L), lambda b, h, c: (b, h, 0, c)),
                  w_in_spec],
        out_specs=[pl.BlockSpec((L, dv), lambda b, h, c: (b * nc + c, h)),
                   pl.BlockSpec((1, 1, dv, dk), lambda b, h, c: (b, h, 0, 0)),
                   pl.BlockSpec((1, 1, 1, dk), lambda b, h, c: (b, h, 0, 0)),
                   pl.BlockSpec((1, 1, 1, LANES), lambda b, h, c: (b, h, 0, 0)),
                   w_out_spec],
        out_shape=[jax.ShapeDtypeStruct((batch * seq, heads * dv), BF16),
                   jax.ShapeDtypeStruct((batch, heads, dv, dk), F32),
                   jax.ShapeDtypeStruct((batch, heads, 1, dk), F32),
                   jax.ShapeDtypeStruct((batch, heads, 1, LANES), F32),
                   w_shape],
        compiler_params=_params(("parallel", "parallel", "arbitrary"), blocks,
                                8 * _nbytes((L, L), F32) + 4 * _nbytes((L, dv), F32)
                                + 3 * _nbytes((dv, dk), F32)),
        name="mlstm_chunkwise",
    )(p1, p1, p1, p2, norm_w, g, b_loc, i_row, b_row, cast_w)


def _mlstm_step_body(q_ref, k_ref, v_ref, og_ref, w_ref, g_ref, c_ref, n_ref, m_ref,
                     y_ref, co_ref, no_ref, mo_ref, *, heads, dk, dv, k_scale):
    g = g_ref[0]
    eye = (lax.broadcasted_iota(jnp.int32, (dv, dv), 0)
           == lax.broadcasted_iota(jnp.int32, (dv, dv), 1)).astype(BF16)
    lane = lax.broadcasted_iota(jnp.int32, (1, LANES), 1)
    m_out = jnp.zeros((1, LANES), F32)
    for h in range(heads):
        q = q_ref[0, :, h * dk:(h + 1) * dk].astype(F32)
        kf = k_ref[0, :, h * dk:(h + 1) * dk].astype(F32) * k_scale
        v = v_ref[0, :, h * dv:(h + 1) * dv].astype(F32)
        ig = g[:, MIG_LANE0 + h:MIG_LANE0 + h + 1]
        lf = g[:, MLF_LANE0 + h:MLF_LANE0 + h + 1]
        m_prev = m_ref[0][:, h:h + 1]
        c_state = c_ref[0, h]
        n_state = n_ref[0, h]
        inter = lf + m_prev
        m_t = jnp.maximum(inter, ig)
        w_intra = jnp.exp(ig - m_t)
        w_inter = jnp.exp(inter - m_t)
        a = w_intra * jnp.sum(q * kf, axis=1, keepdims=True)
        q_rows = jnp.broadcast_to(q, (HEAD_ROWS, dk)).astype(BF16)
        cq = lax.dot_general(q_rows, c_state.astype(BF16), NT_DIMS, preferred_element_type=F32)[0:1]
        num = a * v + w_inter * cq
        den = a + w_inter * jnp.sum(q * n_state, axis=1, keepdims=True)
        hcur = num / jnp.maximum(jnp.abs(den), jnp.exp(-m_t))
        y_ref[0, :, h * dv:(h + 1) * dv] = _head_norm_gate(
            hcur, w_ref[:, h * dv:(h + 1) * dv], og_ref[0, :, h * dv:(h + 1) * dv]).astype(y_ref.dtype)
        v_rows = jnp.broadcast_to(v, (HEAD_ROWS, dv)).astype(BF16)
        v_col = lax.dot_general(eye, v_rows, NT_DIMS, preferred_element_type=F32)[:, 0:1]
        kg = w_intra * kf
        co_ref[0, h] = w_inter * c_state + v_col * kg
        no_ref[0, h] = w_inter * n_state + kg
        m_out = jnp.where(lane == h, m_t, m_out)
    mo_ref[0] = m_out


def _mlstm_step(q, k, v, og, norm_w, g, c0, n0, m0, heads, dk, dv, k_scale):
    db = q.shape[0]
    row = lambda b: (b, 0, 0)
    st = lambda b: (b, 0, 0, 0)
    blocks = 2 * _nbytes((heads, dv, dk), F32) + 4 * _nbytes((1, heads * dv), F32)
    return pl.pallas_call(
        functools.partial(_mlstm_step_body, heads=heads, dk=dk, dv=dv, k_scale=k_scale),
        grid=(db,),
        in_specs=[pl.BlockSpec((1, 1, heads * dk), row), pl.BlockSpec((1, 1, heads * dk), row),
                  pl.BlockSpec((1, 1, heads * dv), row), pl.BlockSpec((1, 1, heads * dv), row),
                  pl.BlockSpec((1, heads * dv), lambda b: (0, 0)),
                  pl.BlockSpec((1, 1, LANES), row),
                  pl.BlockSpec((1, heads, dv, dk), st), pl.BlockSpec((1, heads, 1, dk), st),
                  pl.BlockSpec((1, 1, heads), row)],
        out_specs=[pl.BlockSpec((1, 1, heads * dv), row),
                   pl.BlockSpec((1, heads, dv, dk), st), pl.BlockSpec((1, heads, 1, dk), st),
                   pl.BlockSpec((1, 1, LANES), row)],
        out_shape=[jax.ShapeDtypeStruct((db, 1, heads * dv), BF16),
                   jax.ShapeDtypeStruct((db, heads, dv, dk), F32),
                   jax.ShapeDtypeStruct((db, heads, 1, dk), F32),
                   jax.ShapeDtypeStruct((db, 1, LANES), F32)],
        compiler_params=_params(("parallel",), blocks, 6 * _nbytes((dv, dk), F32)),
        name="mlstm_step",
    )(q, k, v, og, norm_w, g, c0, n0, m0)


def kernel(x_prompt, x_sample, mem_prompt, cache_fox_k, cache_fox_v, cache_fox_logf, state_mlstm_C, state_mlstm_n, state_mlstm_m, cache_mem_k, cache_mem_v, page_table, attn_norm_w, w_in, fox_f_bias, m_i_bias, m_f_bias, m_norm_w, mem_norm_w, w_mem_kv, w_br_fox, w_br_m, w_br_mem, w_out, ffn_norm_w, w_up, w_down, final_norm_w):
    batch, seq, d_model = x_prompt.shape
    dec_batch, dec_seq, _ = x_sample.shape
    depth = w_in.shape[0]
    assert depth == 1 and dec_seq == 1
    fox_heads, fox_dh = cache_fox_k.shape[3], cache_fox_k.shape[4]
    m_heads, m_dv, m_dk = state_mlstm_C.shape[2:]
    n_mem, mem_heads, mem_dh = cache_mem_k.shape[2:]
    fox_w, mqk_w, mv_w, mem_w = fox_heads * fox_dh, m_heads * m_dk, m_heads * m_dv, mem_heads * mem_dh
    assert fox_heads == MIG_LANE0 and m_heads == MLF_LANE0 - MIG_LANE0
    tp, ts = batch * seq, dec_batch * dec_seq
    k_scale = m_dk ** -0.5

    sizes = (fox_w, fox_w, fox_w, fox_heads, mqk_w, mqk_w, mv_w, m_heads, m_heads, mv_w, mem_w,
             3 * d_model)
    offs = [0]
    for s in sizes:
        offs.append(offs[-1] + s)

    l = 0
    w_in_t = jnp.swapaxes(w_in, 1, 2)
    wa_cols, wb_cols, wc_cols = 3 * fox_w, 2 * mqk_w + mv_w, mv_w + mem_w + 3 * d_model
    cast_tk = min(d_model, 2048)
    w_a = _window_cast(w_in_t, l, offs[0], wa_cols, 512, cast_tk)
    w_b = _window_cast(w_in_t, l, offs[4], wb_cols, 512, cast_tk)
    w_c = _window_cast(w_in_t, l, offs[9], wc_cols, 512, cast_tk)
    PB_MQ, PB_MK, PB_MV = 0, mqk_w, 2 * mqk_w
    PC_OG, PC_QQ, PC_GATES = 0, mv_w, mv_w + mem_w
    n_small = fox_heads + 2 * m_heads
    w_sm = _gate_weights(w_in_t, l, offs[3], offs[7], fox_heads, 2 * m_heads)
    bias_sm = jnp.concatenate([fox_f_bias[l], m_i_bias[l], m_f_bias[l],
                               jnp.zeros((LANES - n_small,), F32)]).reshape(1, LANES).astype(F32)
    d_ff = w_up.shape[2]
    m_norm = m_norm_w[l].reshape(1, mv_w).astype(F32)

    def wide_tn(*col_counts):
        return 1024 if all(c % 1024 == 0 for c in col_counts) else 512

    def project(x2d, tm, ride_b=(), ride_c=()):
        h = _rmsnorm(x2d, attn_norm_w[l], BF16, min(tm, 256))
        fq = _matmul(h, w_a, n_cols=fox_w, col_off=0, out_dtype=BF16, tm=tm, tn=wide_tn(fox_w))
        fk = _matmul(h, w_a, n_cols=fox_w, col_off=fox_w, out_dtype=F32, tm=tm, tn=wide_tn(fox_w))
        fv = _matmul(h, w_a, n_cols=fox_w, col_off=2 * fox_w, out_dtype=F32, tm=tm, tn=wide_tn(fox_w))
        pb = _matmul(h, w_b, n_cols=wb_cols, out_dtype=BF16, tm=tm, tn=wide_tn(wb_cols), ride=ride_b)
        pc = _matmul(h, w_c, n_cols=wc_cols, out_dtype=BF16, tm=tm, tn=wide_tn(wc_cols, PC_QQ, PC_GATES),
                     epilogue="sigmoid", plain_cols=(PC_QQ, PC_GATES), ride=ride_c)
        sp = _matmul(h, w_sm, n_cols=LANES, out_dtype=F32, tm=tm, tn=LANES)
        return fq, fk, fv, pb, pc, sp

    def mix_and_mlp(x2d, y_fox, y_m, y_mem, pc, tm):
        merged = _merge(y_fox, y_m, y_mem, w_brf, w_brm, w_brq, pc, PC_GATES, tm, 256)
        x1 = _matmul(merged, w_out_b, n_cols=d_model, out_dtype=F32, tm=tm, tn=512, resid=x2d)
        h2 = _rmsnorm(x1, ffn_norm_w[l], BF16, min(tm, 256))
        up = _matmul(h2, w_up_b, n_cols=d_ff, out_dtype=BF16, tm=tm, tn=wide_tn(d_ff), epilogue="relu2")
        x2 = _matmul(up, w_down_b, n_cols=d_model, out_dtype=F32, tm=min(tm, 512), tn=256, resid=x1)
        return _rmsnorm(x2, final_norm_w, F32, min(tm, 256))

    xp = x_prompt.reshape(tp, d_model)
    fq, fk, fv, (pb, w_memkv_b), (pc, w_brf, w_brm, w_brq, w_out_b), sp = project(
        xp, 1024, ride_b=((w_mem_kv, l),),
        ride_c=((w_br_fox, l), (w_br_m, l), (w_br_mem, l), (w_out, l)))
    g, c_glob, b_loc = _gates(sp.reshape(batch, seq, LANES), bias_sm, M_CHUNK)
    ck = c_glob[:, :, FOX_LANE0:FOX_LANE0 + fox_heads].transpose(0, 2, 1).reshape(batch, fox_heads, 1, seq)
    i_row = g[:, :, MIG_LANE0:MIG_LANE0 + m_heads].transpose(0, 2, 1).reshape(batch, m_heads, 1, seq)
    b_row = b_loc[:, :, MLF_LANE0:MLF_LANE0 + m_heads].transpose(0, 2, 1).reshape(batch, m_heads, 1, seq)

    y_fox, w_down_b = _fox_prompt(fq, 0, fk, fv, c_glob, ck, batch, seq, fox_heads, fox_dh, 512, w_down, l)
    y_m, p_c, p_n, p_m, w_up_b = _mlstm_prompt(pb, PB_MQ, PB_MK, PB_MV, pc, PC_OG, m_norm, g, b_loc, i_row,
                                               b_row, batch, seq, m_heads, m_dk, m_dv, k_scale, w_up, l)
    mem_h = _rmsnorm(mem_prompt.reshape(batch * n_mem, d_model), mem_norm_w[l], BF16, 256)
    mem_k = _matmul(mem_h, w_memkv_b, n_cols=mem_w, col_off=0, out_dtype=F32, tm=batch * n_mem, tn=512)
    mem_v = _matmul(mem_h, w_memkv_b, n_cols=mem_w, col_off=mem_w, out_dtype=F32, tm=batch * n_mem, tn=512)
    y_mem = _mem_prompt(pc, PC_QQ, mem_k, mem_v, batch, seq, n_mem, mem_heads, mem_dh, 512)
    y_prompt = mix_and_mlp(xp, y_fox, y_m, y_mem, pc, 1024).reshape(batch, seq, d_model)

    xs = x_sample.reshape(ts, d_model)
    sfq, sfk, sfv, spb, spc, ssp = project(xs, ts)
    sg = _gates_single(ssp, bias_sm)
    s_flf = sg[:, FOX_LANE0:FOX_LANE0 + fox_heads]
    per_head = lambda a: a.reshape(ts, fox_heads, fox_dh)
    sy_fox = _fox_decode(
        per_head(sfq), per_head(sfk), per_head(sfv), s_flf.reshape(ts, fox_heads, 1),
        cache_fox_k[l], cache_fox_v[l], cache_fox_logf[l].transpose(0, 2, 1), page_table, 8).reshape(ts, fox_w)
    row3 = lambda a: a.reshape(ts, 1, a.shape[-1])
    sy_m, s_c, s_n, s_m = _mlstm_step(
        row3(spb[:, PB_MQ:PB_MQ + mqk_w]), row3(spb[:, PB_MK:PB_MK + mqk_w]), row3(spb[:, PB_MV:PB_MV + mv_w]),
        row3(spc[:, PC_OG:PC_OG + mv_w]), m_norm, row3(sg),
        state_mlstm_C[l], state_mlstm_n[l].reshape(ts, m_heads, 1, m_dk),
        state_mlstm_m[l].reshape(ts, 1, m_heads), m_heads, m_dk, m_dv, k_scale)
    sy_mem = _mem_decode(spc[:, PC_QQ:PC_QQ + mem_w].reshape(ts, mem_heads, mem_dh),
                         cache_mem_k, cache_mem_v, l).reshape(ts, mem_w)
    y_sample = mix_and_mlp(xs, sy_fox, sy_m.reshape(ts, mv_w), sy_mem, spc, ts).reshape(dec_batch, dec_seq, d_model)

    lead = lambda a, shape: a.reshape((1,) + shape)
    return (
        y_prompt, y_sample,
        lead(fk, (batch, seq, fox_heads, fox_dh)), lead(fv, (batch, seq, fox_heads, fox_dh)),
        lead(g[:, :, FOX_LANE0:FOX_LANE0 + fox_heads], (batch, seq, fox_heads)),
        lead(p_c, (batch, m_heads, m_dv, m_dk)), lead(p_n, (batch, m_heads, m_dk)),
        lead(p_m[:, :, 0, 0], (batch, m_heads)),
        lead(mem_k, (batch, n_mem, mem_heads, mem_dh)), lead(mem_v, (batch, n_mem, mem_heads, mem_dh)),
        lead(sfk, (dec_batch, dec_seq, fox_heads, fox_dh)), lead(sfv, (dec_batch, dec_seq, fox_heads, fox_dh)),
        lead(s_flf, (dec_batch, dec_seq, fox_heads)),
        lead(s_c, (dec_batch, m_heads, m_dv, m_dk)), lead(s_n, (dec_batch, m_heads, m_dk)),
        lead(s_m[:, 0, :m_heads], (dec_batch, m_heads)),
    )
```

```python
import functools

import jax
import jax.numpy as jnp
from jax import lax
from jax.experimental import pallas as pl
from jax.experimental.pallas import tpu as pltpu

F32 = jnp.float32
BF16 = jnp.bfloat16
HIGHEST = lax.Precision.HIGHEST

EPS = 1e-6
PAGE_SIZE = 128
LANES = 128
VMEM_LIMIT_CAP = 60000 * 1024
VMEM_SLACK = 6 * 1024 * 1024

FOX_LANE0 = 0
MIG_LANE0 = 16
MLF_LANE0 = 24
M_CHUNK = 256
NT_DIMS = (((1,), (1,)), ((), ()))


def _nbytes(shape, dtype):
    n = 1
    for s in shape:
        n *= s
    return n * jnp.dtype(dtype).itemsize


def _params(semantics, block_bytes, extra_bytes=0):
    limit = min(2 * block_bytes + extra_bytes + VMEM_SLACK, VMEM_LIMIT_CAP)
    return pltpu.CompilerParams(dimension_semantics=semantics, vmem_limit_bytes=int(limit))


def _log_sigmoid(x):
    return jnp.minimum(x, 0.0) - jnp.log1p(jnp.exp(-jnp.abs(x)))


def _sigmoid(x):
    return 1.0 / (1.0 + jnp.exp(-x))


def _rmsnorm_body(x_ref, w_ref, o_ref):
    x = x_ref[...].astype(F32)
    y = x * lax.rsqrt(jnp.mean(x * x, axis=-1, keepdims=True) + EPS)
    o_ref[...] = (y * w_ref[...]).astype(o_ref.dtype)


def _rmsnorm(x, w, out_dtype, tm):
    m, d = x.shape
    blocks = _nbytes((tm, d), x.dtype) + _nbytes((tm, d), out_dtype) + _nbytes((1, d), F32)
    return pl.pallas_call(
        _rmsnorm_body,
        grid=(m // tm,),
        in_specs=[pl.BlockSpec((tm, d), lambda i: (i, 0)),
                  pl.BlockSpec((1, d), lambda i: (0, 0))],
        out_specs=pl.BlockSpec((tm, d), lambda i: (i, 0)),
        out_shape=jax.ShapeDtypeStruct((m, d), out_dtype),
        compiler_params=_params(("parallel",), blocks, _nbytes((tm, d), F32) * 2),
        name="rmsnorm",
    )(x, w.reshape(1, d).astype(F32))


TAIL_ROWS = 32


def _window_cast_body(*refs, shift):
    a_ref, o_ref = refs[0], refs[-1]
    a = a_ref[...]
    if shift:
        wide = jnp.concatenate([a, refs[1][...]], axis=0)
        a = wide[shift:shift + a.shape[0], :]
    o_ref[...] = a.T.astype(o_ref.dtype)


def _window_cast(wt, layer, row0, n_rows, tn, tk):
    kdim = wt.shape[2]
    base = row0 // tn * tn
    shift = row0 - base
    assert kdim % tk == 0 and n_rows % tn == 0 and tn % TAIL_ROWS == 0
    assert shift % 8 == 0 and shift <= TAIL_ROWS and row0 + n_rows <= wt.shape[1]
    jb = base // tn
    in_specs = [pl.BlockSpec((None, tn, tk), lambda j, i: (layer, jb + j, i))]
    args = [wt]
    if shift:
        in_specs.append(pl.BlockSpec((None, TAIL_ROWS, tk),
                                     lambda j, i: (layer, (jb + j + 1) * (tn // TAIL_ROWS), i)))
        args.append(wt)
    blocks = _nbytes((tn + TAIL_ROWS, tk), F32) + _nbytes((tk, tn), BF16)
    return pl.pallas_call(
        functools.partial(_window_cast_body, shift=shift),
        grid=(n_rows // tn, kdim // tk),
        in_specs=in_specs,
        out_specs=pl.BlockSpec((tk, tn), lambda j, i: (i, j)),
        out_shape=jax.ShapeDtypeStruct((kdim, n_rows), BF16),
        compiler_params=_params(("parallel", "parallel"), blocks, 3 * _nbytes((tn + TAIL_ROWS, tk), F32)),
        name="window_cast",
    )(*args)


def _gate_weight_body(f_ref, m_ref, o_ref):
    rows = jnp.concatenate(
        [f_ref[...], m_ref[...],
         jnp.zeros((LANES - f_ref.shape[0] - m_ref.shape[0], f_ref.shape[1]), F32)], axis=0)
    o_ref[...] = rows.T.astype(o_ref.dtype)


def _gate_weights(wt, layer, row_fox, row_m, n_fox, n_m):
    assert row_fox % n_fox == 0 and row_m % n_m == 0 and n_fox % 8 == 0 and n_m % 8 == 0
    kdim = wt.shape[2]
    return pl.pallas_call(
        _gate_weight_body,
        grid=(1,),
        in_specs=[pl.BlockSpec((None, n_fox, kdim), lambda i: (layer, row_fox // n_fox, 0)),
                  pl.BlockSpec((None, n_m, kdim), lambda i: (layer, row_m // n_m, 0))],
        out_specs=pl.BlockSpec((kdim, LANES), lambda i: (0, 0)),
        out_shape=jax.ShapeDtypeStruct((kdim, LANES), BF16),
        name="gate_weights",
    )(wt, wt)


def _ride_cast(w, layer, n_steps, step_of):
    rows, cols = w.shape[1:]
    n_slabs = max(d for d in range(1, n_steps + 1) if rows % d == 0 and (rows // d) % 16 == 0)
    slab = rows // n_slabs
    slab_of = lambda *g: jnp.minimum(step_of(*g), n_slabs - 1)
    in_spec = pl.BlockSpec((None, slab, cols), lambda *g: (layer, slab_of(*g), 0))
    out_spec = pl.BlockSpec((slab, cols), lambda *g: (slab_of(*g), 0))
    block_bytes = _nbytes((slab, cols), F32) + _nbytes((slab, cols), BF16)
    return in_spec, out_spec, jax.ShapeDtypeStruct((rows, cols), BF16), block_bytes


def _mm_body(*refs, nk, epilogue, plain_blocks, has_resid, n_ride):
    x_ref, w_ref = refs[0], refs[1]
    r_ref = refs[2] if has_resid else None
    ride_src = refs[2 + has_resid:2 + has_resid + n_ride]
    o_ref = refs[2 + has_resid + n_ride]
    ride_dst = refs[3 + has_resid + n_ride:3 + has_resid + 2 * n_ride]
    acc_ref = refs[3 + has_resid + 2 * n_ride] if nk > 1 else None
    j = pl.program_id(1)
    for src, dst in zip(ride_src, ride_dst):
        dst[...] = src[...].astype(dst.dtype)
    part = jnp.dot(x_ref[...], w_ref[...], preferred_element_type=F32)

    def finish(acc):
        if epilogue == "sigmoid":
            gated = _sigmoid(acc)
            if plain_blocks is not None:
                gated = jnp.where((j >= plain_blocks[0]) & (j < plain_blocks[1]), acc, gated)
            acc = gated
        elif epilogue == "relu2":
            r = jnp.maximum(acc, 0.0)
            acc = r * r
        if has_resid:
            acc = r_ref[...] + acc
        o_ref[...] = acc.astype(o_ref.dtype)

    if nk == 1:
        finish(part)
    else:
        k = pl.program_id(2)

        @pl.when(k == 0)
        def _():
            acc_ref[...] = part

        @pl.when(k > 0)
        def _():
            acc_ref[...] += part

        @pl.when(k == nk - 1)
        def _():
            finish(acc_ref[...])


def _matmul(x, w, *, n_cols, col_off=0, out_dtype, tm, tn, tk=None, epilogue=None, plain_cols=None,
            resid=None, ride=()):
    m, kdim = x.shape
    tk = kdim if tk is None else tk
    nk = kdim // tk
    assert m % tm == 0 and n_cols % tn == 0 and col_off % tn == 0 and kdim % tk == 0
    joff = col_off // tn
    plain_blocks = None
    if plain_cols is not None:
        assert epilogue == "sigmoid" and plain_cols[0] % tn == 0 and plain_cols[1] % tn == 0
        plain_blocks = (plain_cols[0] // tn, plain_cols[1] // tn)
    in_specs = [pl.BlockSpec((tm, tk), lambda i, j, k: (i, k)),
                pl.BlockSpec((tk, tn), lambda i, j, k: (k, j + joff))]
    args = [x, w]
    blocks = _nbytes((tm, tk), x.dtype) + _nbytes((tk, tn), w.dtype) + _nbytes((tm, tn), out_dtype)
    if resid is not None:
        in_specs.append(pl.BlockSpec((tm, tn), lambda i, j, k: (i, j)))
        args.append(resid)
        blocks += _nbytes((tm, tn), resid.dtype)
    out_specs = [pl.BlockSpec((tm, tn), lambda i, j, k: (i, j))]
    out_shapes = [jax.ShapeDtypeStruct((m, n_cols), out_dtype)]
    nj = n_cols // tn
    for w3d, layer in ride:
        assert nk == 1
        r_in, r_out, r_shape, r_bytes = _ride_cast(w3d, layer, (m // tm) * nj, lambda i, j, k: i * nj + j)
        in_specs.append(r_in)
        args.append(w3d)
        out_specs.append(r_out)
        out_shapes.append(r_shape)
        blocks += r_bytes
    scratch = [pltpu.VMEM((tm, tn), F32)] if nk > 1 else []
    extra = _nbytes((tm, tn), F32) * (3 if nk > 1 else 2)
    outs = pl.pallas_call(
        functools.partial(_mm_body, nk=nk, epilogue=epilogue, plain_blocks=plain_blocks,
                          has_resid=resid is not None, n_ride=len(ride)),
        grid=(m // tm, nj, nk),
        in_specs=in_specs,
        out_specs=out_specs,
        out_shape=out_shapes,
        scratch_shapes=scratch,
        compiler_params=_params(("arbitrary" if ride else "parallel",) * 2 + ("arbitrary",), blocks, extra),
        name="matmul_" + (epilogue or "plain"),
    )(*args)
    return tuple(outs) if ride else outs[0]


def _merge_body(yf_ref, ym_ref, yq_ref, wf_ref, wm_ref, wq_ref, g0_ref, g1_ref, g2_ref, o_ref):
    acc = g0_ref[...].astype(F32) * jnp.dot(yf_ref[...], wf_ref[...], preferred_element_type=F32)
    acc += g1_ref[...].astype(F32) * jnp.dot(ym_ref[...], wm_ref[...], preferred_element_type=F32)
    acc += g2_ref[...].astype(F32) * jnp.dot(yq_ref[...], wq_ref[...], preferred_element_type=F32)
    o_ref[...] = acc.astype(o_ref.dtype)


def _merge(y_fox, y_m, y_mem, w_fox, w_m, w_mem, gates, gate_off, tm, tn):
    t = y_fox.shape[0]
    d = w_fox.shape[1]
    goff = gate_off // tn
    nd = d // tn
    kf, km, kq = y_fox.shape[1], y_m.shape[1], y_mem.shape[1]
    blocks = (_nbytes((tm, kf + km + kq), BF16) + _nbytes((kf + km + kq, tn), BF16)
              + 4 * _nbytes((tm, tn), BF16))
    return pl.pallas_call(
        _merge_body,
        grid=(t // tm, nd),
        in_specs=[pl.BlockSpec((tm, kf), lambda i, j: (i, 0)),
                  pl.BlockSpec((tm, km), lambda i, j: (i, 0)),
                  pl.BlockSpec((tm, kq), lambda i, j: (i, 0)),
                  pl.BlockSpec((kf, tn), lambda i, j: (0, j)),
                  pl.BlockSpec((km, tn), lambda i, j: (0, j)),
                  pl.BlockSpec((kq, tn), lambda i, j: (0, j)),
                  pl.BlockSpec((tm, tn), lambda i, j: (i, goff + j)),
                  pl.BlockSpec((tm, tn), lambda i, j: (i, goff + nd + j)),
                  pl.BlockSpec((tm, tn), lambda i, j: (i, goff + 2 * nd + j))],
        out_specs=pl.BlockSpec((tm, tn), lambda i, j: (i, j)),
        out_shape=jax.ShapeDtypeStruct((t, d), BF16),
        compiler_params=_params(("parallel", "parallel"), blocks, 4 * _nbytes((tm, tn), F32)),
        name="gated_merge",
    )(y_fox, y_m, y_mem, w_fox, w_m, w_mem, gates, gates, gates)


def _gate_values(x):
    lane = lax.broadcasted_iota(jnp.int32, (1, LANES), 1)
    is_ig = (lane >= MIG_LANE0) & (lane < MLF_LANE0)
    return jnp.where(is_ig, x, _log_sigmoid(x))


def _gate_only_body(sp_ref, bias_ref, g_ref):
    g_ref[...] = _gate_values(sp_ref[...] + bias_ref[...])


def _gates_single(sp, bias):
    t = sp.shape[0]
    spec = pl.BlockSpec((t, LANES), lambda i: (0, 0))
    return pl.pallas_call(
        _gate_only_body,
        grid=(1,),
        in_specs=[spec, pl.BlockSpec((1, LANES), lambda i: (0, 0))],
        out_specs=spec,
        out_shape=jax.ShapeDtypeStruct((t, LANES), F32),
        name="small_gates_single",
    )(sp, bias)


def _gate_body(sp_ref, bias_ref, g_ref, cg_ref, bl_ref, *, seq, chunk):
    rows = chunk
    r = lax.broadcasted_iota(jnp.int32, (rows, rows), 0)
    c = lax.broadcasted_iota(jnp.int32, (rows, rows), 1)
    tri = (r >= c).astype(F32)
    carry = jnp.zeros((1, LANES), F32)
    for t in range(seq // rows):
        sl = slice(t * rows, (t + 1) * rows)
        g = _gate_values(sp_ref[0, sl, :] + bias_ref[...])
        g_ref[0, sl, :] = g
        loc = jnp.dot(tri, g, precision=HIGHEST, preferred_element_type=F32)
        bl_ref[0, sl, :] = loc
        cg_ref[0, sl, :] = loc + carry
        carry = carry + loc[rows - 1:rows, :]


def _gates(sp, bias, chunk):
    b, s, _ = sp.shape
    spec = pl.BlockSpec((1, s, LANES), lambda i: (i, 0, 0))
    shape = jax.ShapeDtypeStruct((b, s, LANES), F32)
    return pl.pallas_call(
        functools.partial(_gate_body, seq=s, chunk=chunk),
        grid=(b,),
        in_specs=[spec, pl.BlockSpec((1, LANES), lambda i: (0, 0))],
        out_specs=[spec, spec, spec],
        out_shape=[shape, shape, shape],
        compiler_params=_params(("parallel",), 4 * _nbytes((s, LANES), F32), _nbytes((s, LANES), F32)),
        name="small_gates",
    )(sp, bias)


def _lane_column(tile, lane_index):
    lane = lax.broadcasted_iota(jnp.int32, (1, LANES), 1)
    return jnp.sum(jnp.where(lane == lane_index, tile, 0.0), axis=1, keepdims=True)


def _fox_prompt_body(q_ref, k_ref, v_ref, cq_ref, ck_ref, wsrc_ref, o_ref, wdst_ref, kb_sc, vb_sc, *,
                     scale, tq):
    h = pl.program_id(1)
    seq = q_ref.shape[0]
    wdst_ref[...] = wsrc_ref[...].astype(wdst_ref.dtype)
    kb_sc[...] = k_ref[...].astype(BF16)
    vb_sc[...] = v_ref[...].astype(BF16)
    r = lax.broadcasted_iota(jnp.int32, (tq, tq), 0)
    c = lax.broadcasted_iota(jnp.int32, (tq, tq), 1)
    causal = c <= r
    for i in range(seq // tq):
        lo, hi = i * tq, (i + 1) * tq
        q = q_ref[lo:hi, :]
        cq = _lane_column(cq_ref[0, lo:hi, :], h)

        def scores(a, b):
            s = lax.dot_general(q, kb_sc[a:b, :], NT_DIMS, preferred_element_type=F32) * scale
            return s + cq - ck_ref[0, 0, :, a:b]

        s_d = jnp.where(causal, scores(lo, hi), -jnp.inf)
        m = jnp.max(s_d, axis=1, keepdims=True)
        if i > 0:
            s_o = scores(0, lo)
            m = jnp.maximum(m, jnp.max(s_o, axis=1, keepdims=True))
        p_d = jnp.exp(s_d - m)
        l = jnp.sum(p_d, axis=1, keepdims=True)
        acc = jnp.dot(p_d.astype(BF16), vb_sc[lo:hi, :], preferred_element_type=F32)
        if i > 0:
            p_o = jnp.exp(s_o - m)
            l = l + jnp.sum(p_o, axis=1, keepdims=True)
            acc = acc + jnp.dot(p_o.astype(BF16), vb_sc[0:lo, :], preferred_element_type=F32)
        o_ref[lo:hi, :] = (acc / l).astype(o_ref.dtype)


def _fox_prompt(qsrc, q_col0, k, v, cq, ck, batch, seq, heads, dh, tq, cast_w, cast_layer):
    qoff = q_col0 // dh
    w_in_spec, w_out_spec, w_shape, w_bytes = _ride_cast(
        cast_w, cast_layer, batch * heads, lambda b, h: b * heads + h)
    blocks = (2 * _nbytes((seq, dh), BF16) + 2 * _nbytes((seq, dh), F32)
              + _nbytes((seq, LANES), F32) + _nbytes((8, seq), F32) + w_bytes)
    return pl.pallas_call(
        functools.partial(_fox_prompt_body, scale=dh ** -0.5, tq=tq),
        grid=(batch, heads),
        in_specs=[pl.BlockSpec((seq, dh), lambda b, h: (b, qoff + h)),
                  pl.BlockSpec((seq, dh), lambda b, h: (b, h)),
                  pl.BlockSpec((seq, dh), lambda b, h: (b, h)),
                  pl.BlockSpec((1, seq, LANES), lambda b, h: (b, 0, 0)),
                  pl.BlockSpec((1, 1, 1, seq), lambda b, h: (b, h, 0, 0)),
                  w_in_spec],
        out_specs=[pl.BlockSpec((seq, dh), lambda b, h: (b, h)), w_out_spec],
        out_shape=[jax.ShapeDtypeStruct((batch * seq, heads * dh), BF16), w_shape],
        scratch_shapes=[pltpu.VMEM((seq, dh), BF16), pltpu.VMEM((seq, dh), BF16)],
        compiler_params=_params(("parallel", "parallel"), blocks, 8 * _nbytes((tq, seq), F32)),
        name="fox_prompt_attention",
    )(qsrc, k, v, cq, ck, cast_w)


def _mem_prompt_body(q_ref, k_ref, v_ref, o_ref, *, scale):
    s = lax.dot_general(q_ref[...], k_ref[...].astype(BF16), NT_DIMS,
                        preferred_element_type=F32) * scale
    m = jnp.max(s, axis=1, keepdims=True)
    p = jnp.exp(s - m)
    p = p / jnp.sum(p, axis=1, keepdims=True)
    o_ref[...] = jnp.dot(p.astype(BF16), v_ref[...].astype(BF16),
                         preferred_element_type=F32).astype(o_ref.dtype)


def _mem_prompt(qsrc, q_col0, k, v, batch, seq, n_mem, heads, dh, tq):
    nq = seq // tq
    qoff = q_col0 // dh
    blocks = 2 * _nbytes((tq, dh), BF16) + 2 * _nbytes((n_mem, dh), F32)
    return pl.pallas_call(
        functools.partial(_mem_prompt_body, scale=dh ** -0.5),
        grid=(batch, heads, nq),
        in_specs=[pl.BlockSpec((tq, dh), lambda b, h, i: (b * nq + i, qoff + h)),
                  pl.BlockSpec((n_mem, dh), lambda b, h, i: (b, h)),
                  pl.BlockSpec((n_mem, dh), lambda b, h, i: (b, h))],
        out_specs=pl.BlockSpec((tq, dh), lambda b, h, i: (b * nq + i, h)),
        out_shape=jax.ShapeDtypeStruct((batch * seq, heads * dh), BF16),
        compiler_params=_params(("parallel", "parallel", "parallel"), blocks,
                                4 * _nbytes((tq, n_mem), F32) + _nbytes((tq, dh), F32)),
        name="memory_prompt_attention",
    )(qsrc, k, v)


HEAD_ROWS = 16


def _head_masks(heads, dh):
    assert heads <= HEAD_ROWS
    row = lax.broadcasted_iota(jnp.int32, (HEAD_ROWS, heads * dh), 0)
    col = lax.broadcasted_iota(jnp.int32, (HEAD_ROWS, heads * dh), 1)
    return (col >= row * dh) & (col < (row + 1) * dh)


def _block_diag(row, mask):
    return jnp.where(mask, jnp.broadcast_to(row.astype(F32), mask.shape), 0.0)


def _decode_block(k_blk, v_blk, bias, qbd, scale, m_sc, l_sc, acc_sc):
    s = lax.dot_general(qbd, k_blk.astype(BF16), NT_DIMS, preferred_element_type=F32) * scale
    if bias is not None:
        s = s + bias
    m_prev = m_sc[...]
    m_new = jnp.maximum(m_prev, jnp.max(s, axis=1, keepdims=True))
    alpha = jnp.exp(m_prev - m_new)
    p = jnp.exp(s - m_new)
    l_sc[...] = alpha * l_sc[...] + jnp.sum(p, axis=1, keepdims=True)
    acc_sc[...] = alpha * acc_sc[...] + jnp.dot(p.astype(BF16), v_blk.astype(BF16),
                                               preferred_element_type=F32)
    m_sc[...] = m_new


def _decode_output(mask, l_sc, acc_sc):
    return jnp.sum(jnp.where(mask, acc_sc[...] / l_sc[...], 0.0), axis=0, keepdims=True)


def _split_bf16(x):
    hi = x.astype(BF16)
    r1 = x - hi.astype(F32)
    mid = r1.astype(BF16)
    lo = (r1 - mid.astype(F32)).astype(BF16)
    return hi, mid, lo


def _fox_decode_body(pt_ref, q_ref, kn_ref, vn_ref, lfn_ref, *refs, heads, dh, pages_per_step, scale):
    npg = pages_per_step
    k_refs = refs[0:npg]
    v_refs = refs[npg:2 * npg]
    lf_refs = refs[2 * npg:3 * npg]
    o_ref = refs[3 * npg]
    spread_sc, m_sc, l_sc, acc_sc, carry_sc = refs[3 * npg + 1:]
    j = pl.program_id(1)
    flat = PAGE_SIZE * heads

    @pl.when(j == 0)
    def _():
        m_sc[...] = jnp.sum(q_ref[0].astype(F32) * kn_ref[0], axis=1, keepdims=True) * scale
        l_sc[...] = jnp.ones(l_sc.shape, F32)
        acc_sc[...] = vn_ref[0]
        carry_sc[...] = lfn_ref[0]
        key_of = lax.broadcasted_iota(jnp.int32, (PAGE_SIZE, flat), 1) // heads
        spread_sc[...] = (key_of == lax.broadcasted_iota(jnp.int32, (PAGE_SIZE, flat), 0)).astype(BF16)

    r = lax.broadcasted_iota(jnp.int32, (PAGE_SIZE, PAGE_SIZE), 0)
    c = lax.broadcasted_iota(jnp.int32, (PAGE_SIZE, PAGE_SIZE), 1)
    later = (r > c).astype(F32)
    carry = carry_sc[...]
    pieces = []
    for t in range(npg):
        lf = lf_refs[t][0]
        pieces.extend(_split_bf16(carry + jnp.dot(lf, later, precision=HIGHEST,
                                                  preferred_element_type=F32)))
        carry = carry + jnp.sum(lf, axis=1, keepdims=True)
    carry_sc[...] = carry
    bias_flat = jnp.dot(jnp.concatenate(pieces, axis=0), spread_sc[...], preferred_element_type=F32)

    own_head = (lax.broadcasted_iota(jnp.int32, (heads, flat), 1) % heads
                == lax.broadcasted_iota(jnp.int32, (heads, flat), 0))
    q = q_ref[0]
    scores = []
    for t in range(npg):
        b0 = 3 * heads * t
        bias = (bias_flat[b0:b0 + heads] + bias_flat[b0 + heads:b0 + 2 * heads]
                + bias_flat[b0 + 2 * heads:b0 + 3 * heads])
        k2 = k_refs[t][0].reshape(flat, dh).astype(BF16)
        s = lax.dot_general(q, k2, NT_DIMS, preferred_element_type=F32) * scale + bias
        scores.append(jnp.where(own_head, s, -jnp.inf))
    m_prev = m_sc[...]
    m_new = m_prev
    for s in scores:
        m_new = jnp.maximum(m_new, jnp.max(s, axis=1, keepdims=True))
    alpha = jnp.exp(m_prev - m_new)
    l_new = alpha * l_sc[...]
    acc = alpha * acc_sc[...]
    for t in range(npg):
        p = jnp.exp(scores[t] - m_new)
        l_new = l_new + jnp.sum(p, axis=1, keepdims=True)
        v2 = v_refs[t][0].reshape(flat, dh).astype(BF16)
        acc = acc + jnp.dot(p.astype(BF16), v2, preferred_element_type=F32)
    l_sc[...] = l_new
    acc_sc[...] = acc
    m_sc[...] = m_new

    @pl.when(j == pl.num_programs(1) - 1)
    def _():
        o_ref[0] = (acc_sc[...] / l_sc[...]).astype(o_ref.dtype)


def _fox_decode(q, k_new, v_new, lf_new, pool_k, pool_v, pool_lf_t, page_table, pages_per_step):
    db, heads, dh = q.shape
    assert heads == HEAD_ROWS and pool_k.shape[1] == PAGE_SIZE
    n_pages = page_table.shape[1]
    npg = pages_per_step
    steps = n_pages // npg

    def page_map(t, rank):
        return lambda b, j, pt: (pt[b, n_pages - 1 - (j * npg + t)],) + (0,) * (rank - 1)

    row = lambda b, j, pt: (b, 0, 0)
    in_specs = [pl.BlockSpec((1, heads, dh), row), pl.BlockSpec((1, heads, dh), row),
                pl.BlockSpec((1, heads, dh), row), pl.BlockSpec((1, heads, 1), row)]
    in_specs += [pl.BlockSpec((1, PAGE_SIZE, heads, dh), page_map(t, 4)) for t in range(npg)]
    in_specs += [pl.BlockSpec((1, PAGE_SIZE, heads, dh), page_map(t, 4)) for t in range(npg)]
    in_specs += [pl.BlockSpec((1, heads, PAGE_SIZE), page_map(t, 3)) for t in range(npg)]
    page_bytes = _nbytes((PAGE_SIZE, heads, dh), F32)
    blocks = npg * (2 * page_bytes + _nbytes((heads, PAGE_SIZE), F32))
    return pl.pallas_call(
        functools.partial(_fox_decode_body, heads=heads, dh=dh, pages_per_step=npg, scale=dh ** -0.5),
        grid_spec=pltpu.PrefetchScalarGridSpec(
            num_scalar_prefetch=1,
            grid=(db, steps),
            in_specs=in_specs,
            out_specs=pl.BlockSpec((1, heads, dh), row),
            scratch_shapes=[pltpu.VMEM((PAGE_SIZE, PAGE_SIZE * heads), BF16),
                            pltpu.VMEM((heads, 1), F32), pltpu.VMEM((heads, 1), F32),
                            pltpu.VMEM((heads, dh), F32), pltpu.VMEM((heads, 1), F32)]),
        out_shape=jax.ShapeDtypeStruct((db, heads, dh), BF16),
        compiler_params=_params(("parallel", "arbitrary"), blocks, 4 * page_bytes),
        name="fox_decode_attention",
    )(page_table, q, k_new, v_new, lf_new, *([pool_k] * npg), *([pool_v] * npg), *([pool_lf_t] * npg))


def _mem_decode_body(q_ref, k_ref, v_ref, o_ref, *, heads, scale):
    n_mem, _, dh = k_ref.shape
    flat = n_mem * heads
    q = jnp.concatenate([q_ref[0], jnp.zeros((HEAD_ROWS - heads, dh), q_ref.dtype)], axis=0)
    k2 = k_ref[...].reshape(flat, dh).astype(BF16)
    v2 = v_ref[...].reshape(flat, dh).astype(BF16)
    own_head = (lax.broadcasted_iota(jnp.int32, (HEAD_ROWS, flat), 1) % heads
                == lax.broadcasted_iota(jnp.int32, (HEAD_ROWS, flat), 0) % heads)
    s = lax.dot_general(q, k2, NT_DIMS, preferred_element_type=F32) * scale
    s = jnp.where(own_head, s, -jnp.inf)
    m = jnp.max(s, axis=1, keepdims=True)
    p = jnp.exp(s - m)
    l = jnp.sum(p, axis=1, keepdims=True)
    o = jnp.dot(p.astype(BF16), v2, preferred_element_type=F32)
    o_ref[0] = (o[0:heads] / l[0:heads]).astype(o_ref.dtype)


def _mem_decode(q, k, v, layer):
    db, heads, dh = q.shape
    n_mem = k.shape[2]
    assert heads <= HEAD_ROWS
    cache_spec = pl.BlockSpec((None, None, n_mem, heads, dh), lambda b: (layer, b, 0, 0, 0))
    blocks = 2 * _nbytes((n_mem, 8, dh), F32) + 2 * _nbytes((8, dh), F32)
    return pl.pallas_call(
        functools.partial(_mem_decode_body, heads=heads, scale=dh ** -0.5),
        grid=(db,),
        in_specs=[pl.BlockSpec((1, heads, dh), lambda b: (b, 0, 0)), cache_spec, cache_spec],
        out_specs=pl.BlockSpec((1, heads, dh), lambda b: (b, 0, 0)),
        out_shape=jax.ShapeDtypeStruct((db, heads, dh), BF16),
        compiler_params=_params(("parallel",), blocks, 6 * _nbytes((n_mem * heads, dh), F32)),
        name="memory_decode_attention",
    )(q, k, v)


def _head_norm_gate(h, w_row, og):
    hn = h * lax.rsqrt(jnp.mean(h * h, axis=-1, keepdims=True) + EPS)
    return hn * w_row * og.astype(F32)


def _mlstm_chunk_body(q_ref, k_ref, v_ref, og_ref, w_ref, gcol_ref, bcol_ref, irow_ref, brow_ref, wsrc_ref,
                      y_ref, c_ref, n_ref, m_ref, wdst_ref, *, k_scale, chunk):
    h = pl.program_id(1)
    L = chunk
    wdst_ref[...] = wsrc_ref[...].astype(wdst_ref.dtype)
    c_ref[...] = jnp.zeros(c_ref.shape, F32)
    n_ref[...] = jnp.zeros(n_ref.shape, F32)
    m_prev = jnp.zeros((1, 1), F32)
    r = lax.broadcasted_iota(jnp.int32, (L, L), 0)
    c = lax.broadcasted_iota(jnp.int32, (L, L), 1)
    causal = c <= r
    for ci in range(q_ref.shape[0] // L):
        rows = slice(ci * L, (ci + 1) * L)
        q = q_ref[rows, :]
        kf = k_ref[rows, :].astype(F32) * k_scale
        k = kf.astype(BF16)
        v = v_ref[rows, :]
        c_state = c_ref[0, 0]
        n_state = n_ref[0, 0]
        b_col = _lane_column(bcol_ref[0, rows, :], MLF_LANE0 + h)
        i_col = _lane_column(gcol_ref[0, rows, :], MIG_LANE0 + h)
        b_row = brow_ref[0, 0, :, rows]
        i_row = irow_ref[0, 0, :, rows]

        d = jnp.where(causal, b_col - b_row + i_row, -jnp.inf)
        inter = b_col + m_prev
        m_t = jnp.maximum(inter, jnp.max(d, axis=1, keepdims=True))
        w_intra = jnp.exp(d - m_t)
        w_inter = jnp.exp(inter - m_t)
        a = w_intra * lax.dot_general(q, k, NT_DIMS, preferred_element_type=F32)
        num = (jnp.dot(a.astype(BF16), v, preferred_element_type=F32)
               + w_inter * lax.dot_general(q, c_state.astype(BF16), NT_DIMS, preferred_element_type=F32))
        den = (jnp.sum(a, axis=1, keepdims=True)
               + w_inter * jnp.sum(q.astype(F32) * n_state, axis=1, keepdims=True))
        hcur = num / jnp.maximum(jnp.abs(den), jnp.exp(-m_t))
        y_ref[rows, :] = _head_norm_gate(hcur, w_ref[...], og_ref[rows, :]).astype(y_ref.dtype)

        m_new = m_t[L - 1:L, :]
        b_last = b_col[L - 1:L, :]
        g_inter = jnp.exp(b_last + m_prev - m_new)
        g_intra = jnp.exp(b_last - b_col + i_col - m_new)
        kg = g_intra * kf
        c_ref[0, 0] = g_inter * c_state + lax.dot_general(
            v, kg.astype(BF16), (((0,), (0,)), ((), ())), preferred_element_type=F32)
        n_ref[0, 0] = g_inter * n_state + jnp.sum(kg, axis=0, keepdims=True)
        m_prev = m_new
    m_ref[0, 0] = jnp.broadcast_to(m_prev, (1, LANES))


def _mlstm_prompt(p1, q_col0, k_col0, v_col0, p2, og_col0, norm_w, g, b_loc, i_row, b_row,
                  batch, seq, heads, dk, dv, k_scale, cast_w, cast_layer):
    L = M_CHUNK
    qo, ko, vo, oo = q_col0 // dk, k_col0 // dk, v_col0 // dv, og_col0 // dv
    w_in_spec, w_out_spec, w_shape, w_bytes = _ride_cast(
        cast_w, cast_layer, batch * heads, lambda b, h: b * heads + h)
    blocks = (2 * _nbytes((seq, dk), BF16) + 3 * _nbytes((seq, dv), BF16) + 2 * _nbytes((seq, LANES), F32)
              + _nbytes((dv, dk), F32) + w_bytes)
    return pl.pallas_call(
        functools.partial(_mlstm_chunk_body, k_scale=k_scale, chunk=L),
        grid=(batch, heads),
        in_specs=[pl.BlockSpec((seq, dk), lambda b, h: (b, qo + h)),
                  pl.BlockSpec((seq, dk), lambda b, h: (b, ko + h)),
                  pl.BlockSpec((seq, dv), lambda b, h: (b, vo + h)),
                  pl.BlockSpec((seq, dv), lambda b, h: (b, oo + h)),
                  pl.BlockSpec((1, dv), lambda b, h: (0, h)),
                  pl.BlockSpec((1, seq, LANES), lambda b, h: (b, 0, 0)),
                  pl.BlockSpec((1, seq, LANES), lambda b, h: (b, 0, 0)),
                  pl.BlockSpec((1, 1, 1, seq), lambda b, h: (b, h, 0, 0)),
                  pl.BlockSpec((1, 1, 1, seq), lambda b, h: (b, h, 0, 0)),
                  w_in_spec],
        out_specs=[pl.BlockSpec((seq, dv), lambda b, h: (b, h)),
                   pl.BlockSpec((1, 1, dv, dk), lambda b, h: (b, h, 0, 0)),
                   pl.BlockSpec((1, 1, 1, dk), lambda b, h: (b, h, 0, 0)),
                   pl.BlockSpec((1, 1, 1, LANES), lambda b, h: (b, h, 0, 0)),
                   w_out_spec],
        out_shape=[jax.ShapeDtypeStruct((batch * seq, heads * dv), BF16),
                   jax.ShapeDtypeStruct((batch, heads, dv, dk), F32),
                   jax.ShapeDtypeStruct((batch, heads, 1, dk), F32),
                   jax.ShapeDtypeStruct((batch, heads, 1, LANES), F32),
                   w_shape],
        compiler_params=_params(("parallel", "parallel"), blocks,
                                8 * _nbytes((L, L), F32) + 4 * _nbytes((L, dv), F32)
                                + 3 * _nbytes((dv, dk), F32)),
        name="mlstm_chunkwise",
    )(p1, p1, p1, p2, norm_w, g, b_loc, i_row, b_row, cast_w)


def _mlstm_step_body(q_ref, k_ref, v_ref, og_ref, w_ref, g_ref, c_ref, n_ref, m_ref,
                     y_ref, co_ref, no_ref, mo_ref, *, heads, dk, dv, k_scale):
    g = g_ref[0]
    eye = (lax.broadcasted_iota(jnp.int32, (dv, dv), 0)
           == lax.broadcasted_iota(jnp.int32, (dv, dv), 1)).astype(BF16)
    lane = lax.broadcasted_iota(jnp.int32, (1, LANES), 1)
    m_out = jnp.zeros((1, LANES), F32)
    for h in range(heads):
        q = q_ref[0, :, h * dk:(h + 1) * dk].astype(F32)
        kf = k_ref[0, :, h * dk:(h + 1) * dk].astype(F32) * k_scale
        v = v_ref[0, :, h * dv:(h + 1) * dv].astype(F32)
        ig = g[:, MIG_LANE0 + h:MIG_LANE0 + h + 1]
        lf = g[:, MLF_LANE0 + h:MLF_LANE0 + h + 1]
        m_prev = m_ref[0][:, h:h + 1]
        c_state = c_ref[0, h]
        n_state = n_ref[0, h]
        inter = lf + m_prev
        m_t = jnp.maximum(inter, ig)
        w_intra = jnp.exp(ig - m_t)
        w_inter = jnp.exp(inter - m_t)
        a = w_intra * jnp.sum(q * kf, axis=1, keepdims=True)
        q_rows = jnp.broadcast_to(q, (HEAD_ROWS, dk)).astype(BF16)
        cq = lax.dot_general(q_rows, c_state.astype(BF16), NT_DIMS, preferred_element_type=F32)[0:1]
        num = a * v + w_inter * cq
        den = a + w_inter * jnp.sum(q * n_state, axis=1, keepdims=True)
        hcur = num / jnp.maximum(jnp.abs(den), jnp.exp(-m_t))
        y_ref[0, :, h * dv:(h + 1) * dv] = _head_norm_gate(
            hcur, w_ref[:, h * dv:(h + 1) * dv], og_ref[0, :, h * dv:(h + 1) * dv]).astype(y_ref.dtype)
        v_rows = jnp.broadcast_to(v, (HEAD_ROWS, dv)).astype(BF16)
        v_col = lax.dot_general(eye, v_rows, NT_DIMS, preferred_element_type=F32)[:, 0:1]
        kg = w_intra * kf
        co_ref[0, h] = w_inter * c_state + v_col * kg
        no_ref[0, h] = w_inter * n_state + kg
        m_out = jnp.where(lane == h, m_t, m_out)
    mo_ref[0] = m_out


def _mlstm_step(q, k, v, og, norm_w, g, c0, n0, m0, heads, dk, dv, k_scale):
    db = q.shape[0]
    row = lambda b: (b, 0, 0)
    st = lambda b: (b, 0, 0, 0)
    blocks = 2 * _nbytes((heads, dv, dk), F32) + 4 * _nbytes((1, heads * dv), F32)
    return pl.pallas_call(
        functools.partial(_mlstm_step_body, heads=heads, dk=dk, dv=dv, k_scale=k_scale),
        grid=(db,),
        in_specs=[pl.BlockSpec((1, 1, heads * dk), row), pl.BlockSpec((1, 1, heads * dk), row),
                  pl.BlockSpec((1, 1, heads * dv), row), pl.BlockSpec((1, 1, heads * dv), row),
                  pl.BlockSpec((1, heads * dv), lambda b: (0, 0)),
                  pl.BlockSpec((1, 1, LANES), row),
                  pl.BlockSpec((1, heads, dv, dk), st), pl.BlockSpec((1, heads, 1, dk), st),
                  pl.BlockSpec((1, 1, heads), row)],
        out_specs=[pl.BlockSpec((1, 1, heads * dv), row),
                   pl.BlockSpec((1, heads, dv, dk), st), pl.BlockSpec((1, heads, 1, dk), st),
                   pl.BlockSpec((1, 1, LANES), row)],
        out_shape=[jax.ShapeDtypeStruct((db, 1, heads * dv), BF16),
                   jax.ShapeDtypeStruct((db, heads, dv, dk), F32),
                   jax.ShapeDtypeStruct((db, heads, 1, dk), F32),
                   jax.ShapeDtypeStruct((db, 1, LANES), F32)],
        compiler_params=_params(("parallel",), blocks, 6 * _nbytes((dv, dk), F32)),
        name="mlstm_step",
    )(q, k, v, og, norm_w, g, c0, n0, m0)


def kernel(x_prompt, x_sample, mem_prompt, cache_fox_k, cache_fox_v, cache_fox_logf, state_mlstm_C, state_mlstm_n, state_mlstm_m, cache_mem_k, cache_mem_v, page_table, attn_norm_w, w_in, fox_f_bias, m_i_bias, m_f_bias, m_norm_w, mem_norm_w, w_mem_kv, w_br_fox, w_br_m, w_br_mem, w_out, ffn_norm_w, w_up, w_down, final_norm_w):
    batch, seq, d_model = x_prompt.shape
    dec_batch, dec_seq, _ = x_sample.shape
    depth = w_in.shape[0]
    assert depth == 1 and dec_seq == 1
    fox_heads, fox_dh = cache_fox_k.shape[3], cache_fox_k.shape[4]
    m_heads, m_dv, m_dk = state_mlstm_C.shape[2:]
    n_mem, mem_heads, mem_dh = cache_mem_k.shape[2:]
    fox_w, mqk_w, mv_w, mem_w = fox_heads * fox_dh, m_heads * m_dk, m_heads * m_dv, mem_heads * mem_dh
    assert fox_heads == MIG_LANE0 and m_heads == MLF_LANE0 - MIG_LANE0
    tp, ts = batch * seq, dec_batch * dec_seq
    k_scale = m_dk ** -0.5

    sizes = (fox_w, fox_w, fox_w, fox_heads, mqk_w, mqk_w, mv_w, m_heads, m_heads, mv_w, mem_w,
             3 * d_model)
    offs = [0]
    for s in sizes:
        offs.append(offs[-1] + s)

    l = 0
    w_in_t = jnp.swapaxes(w_in, 1, 2)
    wa_cols, wb_cols, wc_cols = 3 * fox_w, 2 * mqk_w + mv_w, mv_w + mem_w + 3 * d_model
    cast_tk = min(d_model, 2048)
    w_a = _window_cast(w_in_t, l, offs[0], wa_cols, 512, cast_tk)
    w_b = _window_cast(w_in_t, l, offs[4], wb_cols, 512, cast_tk)
    w_c = _window_cast(w_in_t, l, offs[9], wc_cols, 512, cast_tk)
    PB_MQ, PB_MK, PB_MV = 0, mqk_w, 2 * mqk_w
    PC_OG, PC_QQ, PC_GATES = 0, mv_w, mv_w + mem_w
    n_small = fox_heads + 2 * m_heads
    w_sm = _gate_weights(w_in_t, l, offs[3], offs[7], fox_heads, 2 * m_heads)
    bias_sm = jnp.concatenate([fox_f_bias[l], m_i_bias[l], m_f_bias[l],
                               jnp.zeros((LANES - n_small,), F32)]).reshape(1, LANES).astype(F32)
    d_ff = w_up.shape[2]
    m_norm = m_norm_w[l].reshape(1, mv_w).astype(F32)

    def wide_tn(*col_counts):
        return 1024 if all(c % 1024 == 0 for c in col_counts) else 512

    def project(x2d, tm, ride_b=(), ride_c=()):
        h = _rmsnorm(x2d, attn_norm_w[l], BF16, min(tm, 256))
        fq = _matmul(h, w_a, n_cols=fox_w, col_off=0, out_dtype=BF16, tm=tm, tn=wide_tn(fox_w))
        fk = _matmul(h, w_a, n_cols=fox_w, col_off=fox_w, out_dtype=F32, tm=tm, tn=wide_tn(fox_w))
        fv = _matmul(h, w_a, n_cols=fox_w, col_off=2 * fox_w, out_dtype=F32, tm=tm, tn=wide_tn(fox_w))
        pb = _matmul(h, w_b, n_cols=wb_cols, out_dtype=BF16, tm=tm, tn=wide_tn(wb_cols), ride=ride_b)
        pc = _matmul(h, w_c, n_cols=wc_cols, out_dtype=BF16, tm=tm, tn=wide_tn(wc_cols, PC_QQ, PC_GATES),
                     epilogue="sigmoid", plain_cols=(PC_QQ, PC_GATES), ride=ride_c)
        sp = _matmul(h, w_sm, n_cols=LANES, out_dtype=F32, tm=tm, tn=LANES)
        return fq, fk, fv, pb, pc, sp

    def mix_and_mlp(x2d, y_fox, y_m, y_mem, pc, tm):
        merged = _merge(y_fox, y_m, y_mem, w_brf, w_brm, w_brq, pc, PC_GATES, tm, 256)
        x1 = _matmul(merged, w_out_b, n_cols=d_model, out_dtype=F32, tm=tm, tn=512, resid=x2d)
        h2 = _rmsnorm(x1, ffn_norm_w[l], BF16, min(tm, 256))
        up = _matmul(h2, w_up_b, n_cols=d_ff, out_dtype=BF16, tm=tm, tn=wide_tn(d_ff), epilogue="relu2")
        x2 = _matmul(up, w_down_b, n_cols=d_model, out_dtype=F32, tm=min(tm, 512), tn=256, resid=x1)
        return _rmsnorm(x2, final_norm_w, F32, min(tm, 256))

    xp = x_prompt.reshape(tp, d_model)
    fq, fk, fv, (pb, w_memkv_b), (pc, w_brf, w_brm, w_brq, w_out_b), sp = project(
        xp, 1024, ride_b=((w_mem_kv, l),),
        ride_c=((w_br_fox, l), (w_br_m, l), (w_br_mem, l), (w_out, l)))
    g, c_glob, b_loc = _gates(sp.reshape(batch, seq, LANES), bias_sm, M_CHUNK)
    ck = c_glob[:, :, FOX_LANE0:FOX_LANE0 + fox_heads].transpose(0, 2, 1).reshape(batch, fox_heads, 1, seq)
    i_row = g[:, :, MIG_LANE0:MIG_LANE0 + m_heads].transpose(0, 2, 1).reshape(batch, m_heads, 1, seq)
    b_row = b_loc[:, :, MLF_LANE0:MLF_LANE0 + m_heads].transpose(0, 2, 1).reshape(batch, m_heads, 1, seq)

    y_fox, w_down_b = _fox_prompt(fq, 0, fk, fv, c_glob, ck, batch, seq, fox_heads, fox_dh, 512, w_down, l)
    y_m, p_c, p_n, p_m, w_up_b = _mlstm_prompt(pb, PB_MQ, PB_MK, PB_MV, pc, PC_OG, m_norm, g, b_loc, i_row,
                                               b_row, batch, seq, m_heads, m_dk, m_dv, k_scale, w_up, l)
    mem_h = _rmsnorm(mem_prompt.reshape(batch * n_mem, d_model), mem_norm_w[l], BF16, 256)
    mem_k = _matmul(mem_h, w_memkv_b, n_cols=mem_w, col_off=0, out_dtype=F32, tm=batch * n_mem, tn=512)
    mem_v = _matmul(mem_h, w_memkv_b, n_cols=mem_w, col_off=mem_w, out_dtype=F32, tm=batch * n_mem, tn=512)
    y_mem = _mem_prompt(pc, PC_QQ, mem_k, mem_v, batch, seq, n_mem, mem_heads, mem_dh, 512)
    y_prompt = mix_and_mlp(xp, y_fox, y_m, y_mem, pc, 1024).reshape(batch, seq, d_model)

    xs = x_sample.reshape(ts, d_model)
    sfq, sfk, sfv, spb, spc, ssp = project(xs, ts)
    sg = _gates_single(ssp, bias_sm)
    s_flf = sg[:, FOX_LANE0:FOX_LANE0 + fox_heads]
    per_head = lambda a: a.reshape(ts, fox_heads, fox_dh)
    sy_fox = _fox_decode(
        per_head(sfq), per_head(sfk), per_head(sfv), s_flf.reshape(ts, fox_heads, 1),
        cache_fox_k[l], cache_fox_v[l], cache_fox_logf[l].transpose(0, 2, 1), page_table, 8).reshape(ts, fox_w)
    row3 = lambda a: a.reshape(ts, 1, a.shape[-1])
    sy_m, s_c, s_n, s_m = _mlstm_step(
        row3(spb[:, PB_MQ:PB_MQ + mqk_w]), row3(spb[:, PB_MK:PB_MK + mqk_w]), row3(spb[:, PB_MV:PB_MV + mv_w]),
        row3(spc[:, PC_OG:PC_OG + mv_w]), m_norm, row3(sg),
        state_mlstm_C[l], state_mlstm_n[l].reshape(ts, m_heads, 1, m_dk),
        state_mlstm_m[l].reshape(ts, 1, m_heads), m_heads, m_dk, m_dv, k_scale)
    sy_mem = _mem_decode(spc[:, PC_QQ:PC_QQ + mem_w].reshape(ts, mem_heads, mem_dh),
                         cache_mem_k, cache_mem_v, l).reshape(ts, mem_w)
    y_sample = mix_and_mlp(xs, sy_fox, sy_m.reshape(ts, mv_w), sy_mem, spc, ts).reshape(dec_batch, dec_seq, d_model)

    lead = lambda a, shape: a.reshape((1,) + shape)
    return (
        y_prompt, y_sample,
        lead(fk, (batch, seq, fox_heads, fox_dh)), lead(fv, (batch, seq, fox_heads, fox_dh)),
        lead(g[:, :, FOX_LANE0:FOX_LANE0 + fox_heads], (batch, seq, fox_heads)),
        lead(p_c, (batch, m_heads, m_dv, m_dk)), lead(p_n, (batch, m_heads, m_dk)),
        lead(p_m[:, :, 0, 0], (batch, m_heads)),
        lead(mem_k, (batch, n_mem, mem_heads, mem_dh)), lead(mem_v, (batch, n_mem, mem_heads, mem_dh)),
        lead(sfk, (dec_batch, dec_seq, fox_heads, fox_dh)), lead(sfv, (dec_batch, dec_seq, fox_heads, fox_dh)),
        lead(s_flf, (dec_batch, dec_seq, fox_heads)),
        lead(s_c, (dec_batch, m_heads, m_dv, m_dk)), lead(s_n, (dec_batch, m_heads, m_dk)),
        lead(s_m[:, 0, :m_heads], (dec_batch, m_heads)),
    )
```

```python
import functools

import jax
import jax.numpy as jnp
from jax import lax
from jax.experimental import pallas as pl
from jax.experimental.pallas import tpu as pltpu

F32 = jnp.float32
BF16 = jnp.bfloat16
HIGHEST = lax.Precision.HIGHEST

EPS = 1e-6
PAGE_SIZE = 128
LANES = 128
VMEM_LIMIT_CAP = 60000 * 1024
VMEM_SLACK = 6 * 1024 * 1024

FOX_LANE0 = 0
MIG_LANE0 = 16
MLF_LANE0 = 24
M_CHUNK = 256
NT_DIMS = (((1,), (1,)), ((), ()))


def _nbytes(shape, dtype):
    n = 1
    for s in shape:
        n *= s
    return n * jnp.dtype(dtype).itemsize


def _params(semantics, block_bytes, extra_bytes=0):
    limit = min(2 * block_bytes + extra_bytes + VMEM_SLACK, VMEM_LIMIT_CAP)
    return pltpu.CompilerParams(dimension_semantics=semantics, vmem_limit_bytes=int(limit))


def _log_sigmoid(x):
    return jnp.minimum(x, 0.0) - jnp.log1p(jnp.exp(-jnp.abs(x)))


def _sigmoid(x):
    return 1.0 / (1.0 + jnp.exp(-x))


def _rmsnorm_body(x_ref, w_ref, o_ref):
    x = x_ref[...].astype(F32)
    y = x * lax.rsqrt(jnp.mean(x * x, axis=-1, keepdims=True) + EPS)
    o_ref[...] = (y * w_ref[...]).astype(o_ref.dtype)


def _rmsnorm(x, w, out_dtype, tm):
    m, d = x.shape
    blocks = _nbytes((tm, d), x.dtype) + _nbytes((tm, d), out_dtype) + _nbytes((1, d), F32)
    return pl.pallas_call(
        _rmsnorm_body,
        grid=(m // tm,),
        in_specs=[pl.BlockSpec((tm, d), lambda i: (i, 0)),
                  pl.BlockSpec((1, d), lambda i: (0, 0))],
        out_specs=pl.BlockSpec((tm, d), lambda i: (i, 0)),
        out_shape=jax.ShapeDtypeStruct((m, d), out_dtype),
        compiler_params=_params(("parallel",), blocks, _nbytes((tm, d), F32) * 2),
        name="rmsnorm",
    )(x, w.reshape(1, d).astype(F32))


TAIL_ROWS = 32


def _window_cast_body(*refs, shift):
    a_ref, o_ref = refs[0], refs[-1]
    a = a_ref[...]
    if shift:
        wide = jnp.concatenate([a, refs[1][...]], axis=0)
        a = wide[shift:shift + a.shape[0], :]
    o_ref[...] = a.T.astype(o_ref.dtype)


def _window_cast(wt, layer, row0, n_rows, tn, tk):
    kdim = wt.shape[2]
    base = row0 // tn * tn
    shift = row0 - base
    assert kdim % tk == 0 and n_rows % tn == 0 and tn % TAIL_ROWS == 0
    assert shift % 8 == 0 and shift <= TAIL_ROWS and row0 + n_rows <= wt.shape[1]
    jb = base // tn
    in_specs = [pl.BlockSpec((None, tn, tk), lambda j, i: (layer, jb + j, i))]
    args = [wt]
    if shift:
        in_specs.append(pl.BlockSpec((None, TAIL_ROWS, tk),
                                     lambda j, i: (layer, (jb + j + 1) * (tn // TAIL_ROWS), i)))
        args.append(wt)
    blocks = _nbytes((tn + TAIL_ROWS, tk), F32) + _nbytes((tk, tn), BF16)
    return pl.pallas_call(
        functools.partial(_window_cast_body, shift=shift),
        grid=(n_rows // tn, kdim // tk),
        in_specs=in_specs,
        out_specs=pl.BlockSpec((tk, tn), lambda j, i: (i, j)),
        out_shape=jax.ShapeDtypeStruct((kdim, n_rows), BF16),
        compiler_params=_params(("parallel", "parallel"), blocks, 3 * _nbytes((tn + TAIL_ROWS, tk), F32)),
        name="window_cast",
    )(*args)


def _gate_weight_body(f_ref, m_ref, o_ref):
    rows = jnp.concatenate(
        [f_ref[...], m_ref[...],
         jnp.zeros((LANES - f_ref.shape[0] - m_ref.shape[0], f_ref.shape[1]), F32)], axis=0)
    o_ref[...] = rows.T.astype(o_ref.dtype)


def _gate_weights(wt, layer, row_fox, row_m, n_fox, n_m):
    assert row_fox % n_fox == 0 and row_m % n_m == 0 and n_fox % 8 == 0 and n_m % 8 == 0
    kdim = wt.shape[2]
    return pl.pallas_call(
        _gate_weight_body,
        grid=(1,),
        in_specs=[pl.BlockSpec((None, n_fox, kdim), lambda i: (layer, row_fox // n_fox, 0)),
                  pl.BlockSpec((None, n_m, kdim), lambda i: (layer, row_m // n_m, 0))],
        out_specs=pl.BlockSpec((kdim, LANES), lambda i: (0, 0)),
        out_shape=jax.ShapeDtypeStruct((kdim, LANES), BF16),
        name="gate_weights",
    )(wt, wt)


def _ride_cast(w, layer, n_steps, step_of):
    rows, cols = w.shape[1:]
    n_slabs = max(d for d in range(1, n_steps + 1) if rows % d == 0 and (rows // d) % 16 == 0)
    slab = rows // n_slabs
    slab_of = lambda *g: jnp.minimum(step_of(*g), n_slabs - 1)
    in_spec = pl.BlockSpec((None, slab, cols), lambda *g: (layer, slab_of(*g), 0))
    out_spec = pl.BlockSpec((slab, cols), lambda *g: (slab_of(*g), 0))
    block_bytes = _nbytes((slab, cols), F32) + _nbytes((slab, cols), BF16)
    return in_spec, out_spec, jax.ShapeDtypeStruct((rows, cols), BF16), block_bytes


def _mm_body(*refs, nk, epilogue, plain_blocks, has_resid, n_ride):
    x_ref, w_ref = refs[0], refs[1]
    r_ref = refs[2] if has_resid else None
    ride_src = refs[2 + has_resid:2 + has_resid + n_ride]
    o_ref = refs[2 + has_resid + n_ride]
    ride_dst = refs[3 + has_resid + n_ride:3 + has_resid + 2 * n_ride]
    acc_ref = refs[3 + has_resid + 2 * n_ride] if nk > 1 else None
    j = pl.program_id(1)
    for src, dst in zip(ride_src, ride_dst):
        dst[...] = src[...].astype(dst.dtype)
    part = jnp.dot(x_ref[...], w_ref[...], preferred_element_type=F32)

    def finish(acc):
        if epilogue == "sigmoid":
            gated = _sigmoid(acc)
            if plain_blocks is not None:
                gated = jnp.where((j >= plain_blocks[0]) & (j < plain_blocks[1]), acc, gated)
            acc = gated
        elif epilogue == "relu2":
            r = jnp.maximum(acc, 0.0)
            acc = r * r
        if has_resid:
            acc = r_ref[...] + acc
        o_ref[...] = acc.astype(o_ref.dtype)

    if nk == 1:
        finish(part)
    else:
        k = pl.program_id(2)

        @pl.when(k == 0)
        def _():
            acc_ref[...] = part

        @pl.when(k > 0)
        def _():
            acc_ref[...] += part

        @pl.when(k == nk - 1)
        def _():
            finish(acc_ref[...])


def _matmul(x, w, *, n_cols, col_off=0, out_dtype, tm, tn, tk=None, epilogue=None, plain_cols=None,
            resid=None, ride=()):
    m, kdim = x.shape
    tk = kdim if tk is None else tk
    nk = kdim // tk
    assert m % tm == 0 and n_cols % tn == 0 and col_off % tn == 0 and kdim % tk == 0
    joff = col_off // tn
    plain_blocks = None
    if plain_cols is not None:
        assert epilogue == "sigmoid" and plain_cols[0] % tn == 0 and plain_cols[1] % tn == 0
        plain_blocks = (plain_cols[0] // tn, plain_cols[1] // tn)
    in_specs = [pl.BlockSpec((tm, tk), lambda i, j, k: (i, k)),
                pl.BlockSpec((tk, tn), lambda i, j, k: (k, j + joff))]
    args = [x, w]
    blocks = _nbytes((tm, tk), x.dtype) + _nbytes((tk, tn), w.dtype) + _nbytes((tm, tn), out_dtype)
    if resid is not None:
        in_specs.append(pl.BlockSpec((tm, tn), lambda i, j, k: (i, j)))
        args.append(resid)
        blocks += _nbytes((tm, tn), resid.dtype)
    out_specs = [pl.BlockSpec((tm, tn), lambda i, j, k: (i, j))]
    out_shapes = [jax.ShapeDtypeStruct((m, n_cols), out_dtype)]
    nj = n_cols // tn
    for w3d, layer in ride:
        assert nk == 1
        r_in, r_out, r_shape, r_bytes = _ride_cast(w3d, layer, (m // tm) * nj, lambda i, j, k: i * nj + j)
        in_specs.append(r_in)
        args.append(w3d)
        out_specs.append(r_out)
        out_shapes.append(r_shape)
        blocks += r_bytes
    scratch = [pltpu.VMEM((tm, tn), F32)] if nk > 1 else []
    extra = _nbytes((tm, tn), F32) * (3 if nk > 1 else 2)
    outs = pl.pallas_call(
        functools.partial(_mm_body, nk=nk, epilogue=epilogue, plain_blocks=plain_blocks,
                          has_resid=resid is not None, n_ride=len(ride)),
        grid=(m // tm, nj, nk),
        in_specs=in_specs,
        out_specs=out_specs,
        out_shape=out_shapes,
        scratch_shapes=scratch,
        compiler_params=_params(("arbitrary" if ride else "parallel",) * 2 + ("arbitrary",), blocks, extra),
        name="matmul_" + (epilogue or "plain"),
    )(*args)
    return tuple(outs) if ride else outs[0]


def _merge_body(yf_ref, ym_ref, yq_ref, wf_ref, wm_ref, wq_ref, g0_ref, g1_ref, g2_ref, o_ref):
    acc = g0_ref[...].astype(F32) * jnp.dot(yf_ref[...], wf_ref[...], preferred_element_type=F32)
    acc += g1_ref[...].astype(F32) * jnp.dot(ym_ref[...], wm_ref[...], preferred_element_type=F32)
    acc += g2_ref[...].astype(F32) * jnp.dot(yq_ref[...], wq_ref[...], preferred_element_type=F32)
    o_ref[...] = acc.astype(o_ref.dtype)


def _merge(y_fox, y_m, y_mem, w_fox, w_m, w_mem, gates, gate_off, tm, tn):
    t = y_fox.shape[0]
    d = w_fox.shape[1]
    goff = gate_off // tn
    nd = d // tn
    kf, km, kq = y_fox.shape[1], y_m.shape[1], y_mem.shape[1]
    blocks = (_nbytes((tm, kf + km + kq), BF16) + _nbytes((kf + km + kq, tn), BF16)
              + 4 * _nbytes((tm, tn), BF16))
    return pl.pallas_call(
        _merge_body,
        grid=(t // tm, nd),
        in_specs=[pl.BlockSpec((tm, kf), lambda i, j: (i, 0)),
                  pl.BlockSpec((tm, km), lambda i, j: (i, 0)),
                  pl.BlockSpec((tm, kq), lambda i, j: (i, 0)),
                  pl.BlockSpec((kf, tn), lambda i, j: (0, j)),
                  pl.BlockSpec((km, tn), lambda i, j: (0, j)),
                  pl.BlockSpec((kq, tn), lambda i, j: (0, j)),
                  pl.BlockSpec((tm, tn), lambda i, j: (i, goff + j)),
                  pl.BlockSpec((tm, tn), lambda i, j: (i, goff + nd + j)),
                  pl.BlockSpec((tm, tn), lambda i, j: (i, goff + 2 * nd + j))],
        out_specs=pl.BlockSpec((tm, tn), lambda i, j: (i, j)),
        out_shape=jax.ShapeDtypeStruct((t, d), BF16),
        compiler_params=_params(("parallel", "parallel"), blocks, 4 * _nbytes((tm, tn), F32)),
        name="gated_merge",
    )(y_fox, y_m, y_mem, w_fox, w_m, w_mem, gates, gates, gates)


def _gate_values(x):
    lane = lax.broadcasted_iota(jnp.int32, (1, LANES), 1)
    is_ig = (lane >= MIG_LANE0) & (lane < MLF_LANE0)
    return jnp.where(is_ig, x, _log_sigmoid(x))


def _gate_only_body(sp_ref, bias_ref, g_ref):
    g_ref[...] = _gate_values(sp_ref[...] + bias_ref[...])


def _gates_single(sp, bias):
    t = sp.shape[0]
    spec = pl.BlockSpec((t, LANES), lambda i: (0, 0))
    return pl.pallas_call(
        _gate_only_body,
        grid=(1,),
        in_specs=[spec, pl.BlockSpec((1, LANES), lambda i: (0, 0))],
        out_specs=spec,
        out_shape=jax.ShapeDtypeStruct((t, LANES), F32),
        name="small_gates_single",
    )(sp, bias)


def _gate_body(sp_ref, bias_ref, g_ref, cg_ref, bl_ref, *, seq, chunk):
    rows = chunk
    r = lax.broadcasted_iota(jnp.int32, (rows, rows), 0)
    c = lax.broadcasted_iota(jnp.int32, (rows, rows), 1)
    tri = (r >= c).astype(F32)
    carry = jnp.zeros((1, LANES), F32)
    for t in range(seq // rows):
        sl = slice(t * rows, (t + 1) * rows)
        g = _gate_values(sp_ref[0, sl, :] + bias_ref[...])
        g_ref[0, sl, :] = g
        loc = jnp.dot(tri, g, precision=HIGHEST, preferred_element_type=F32)
        bl_ref[0, sl, :] = loc
        cg_ref[0, sl, :] = loc + carry
        carry = carry + loc[rows - 1:rows, :]


def _gates(sp, bias, chunk):
    b, s, _ = sp.shape
    spec = pl.BlockSpec((1, s, LANES), lambda i: (i, 0, 0))
    shape = jax.ShapeDtypeStruct((b, s, LANES), F32)
    return pl.pallas_call(
        functools.partial(_gate_body, seq=s, chunk=chunk),
        grid=(b,),
        in_specs=[spec, pl.BlockSpec((1, LANES), lambda i: (0, 0))],
        out_specs=[spec, spec, spec],
        out_shape=[shape, shape, shape],
        compiler_params=_params(("parallel",), 4 * _nbytes((s, LANES), F32), _nbytes((s, LANES), F32)),
        name="small_gates",
    )(sp, bias)


def _lane_column(tile, lane_index):
    lane = lax.broadcasted_iota(jnp.int32, (1, LANES), 1)
    return jnp.sum(jnp.where(lane == lane_index, tile, 0.0), axis=1, keepdims=True)


def _fox_prompt_body(q_ref, k_ref, v_ref, cq_ref, ck_ref, wsrc_ref, o_ref, wdst_ref, kb_sc, vb_sc, *,
                     scale, tq):
    h = pl.program_id(1)
    seq = q_ref.shape[0]
    wdst_ref[...] = wsrc_ref[...].astype(wdst_ref.dtype)
    kb_sc[...] = k_ref[...].astype(BF16)
    vb_sc[...] = v_ref[...].astype(BF16)
    r = lax.broadcasted_iota(jnp.int32, (tq, tq), 0)
    c = lax.broadcasted_iota(jnp.int32, (tq, tq), 1)
    causal = c <= r
    for i in range(seq // tq):
        lo, hi = i * tq, (i + 1) * tq
        q = q_ref[lo:hi, :]
        cq = _lane_column(cq_ref[0, lo:hi, :], h)

        def scores(a, b):
            s = lax.dot_general(q, kb_sc[a:b, :], NT_DIMS, preferred_element_type=F32) * scale
            return s + cq - ck_ref[0, 0, :, a:b]

        s_d = jnp.where(causal, scores(lo, hi), -jnp.inf)
        m = jnp.max(s_d, axis=1, keepdims=True)
        if i > 0:
            s_o = scores(0, lo)
            m = jnp.maximum(m, jnp.max(s_o, axis=1, keepdims=True))
        p_d = jnp.exp(s_d - m)
        l = jnp.sum(p_d, axis=1, keepdims=True)
        acc = jnp.dot(p_d.astype(BF16), vb_sc[lo:hi, :], preferred_element_type=F32)
        if i > 0:
            p_o = jnp.exp(s_o - m)
            l = l + jnp.sum(p_o, axis=1, keepdims=True)
            acc = acc + jnp.dot(p_o.astype(BF16), vb_sc[0:lo, :], preferred_element_type=F32)
        o_ref[lo:hi, :] = (acc / l).astype(o_ref.dtype)


def _fox_prompt(qsrc, q_col0, k, v, cq, ck, batch, seq, heads, dh, tq, cast_w, cast_layer):
    qoff = q_col0 // dh
    w_in_spec, w_out_spec, w_shape, w_bytes = _ride_cast(
        cast_w, cast_layer, batch * heads, lambda b, h: b * heads + h)
    blocks = (2 * _nbytes((seq, dh), BF16) + 2 * _nbytes((seq, dh), F32)
              + _nbytes((seq, LANES), F32) + _nbytes((8, seq), F32) + w_bytes)
    return pl.pallas_call(
        functools.partial(_fox_prompt_body, scale=dh ** -0.5, tq=tq),
        grid=(batch, heads),
        in_specs=[pl.BlockSpec((seq, dh), lambda b, h: (b, qoff + h)),
                  pl.BlockSpec((seq, dh), lambda b, h: (b, h)),
                  pl.BlockSpec((seq, dh), lambda b, h: (b, h)),
                  pl.BlockSpec((1, seq, LANES), lambda b, h: (b, 0, 0)),
                  pl.BlockSpec((1, 1, 1, seq), lambda b, h: (b, h, 0, 0)),
                  w_in_spec],
        out_specs=[pl.BlockSpec((seq, dh), lambda b, h: (b, h)), w_out_spec],
        out_shape=[jax.ShapeDtypeStruct((batch * seq, heads * dh), BF16), w_shape],
        scratch_shapes=[pltpu.VMEM((seq, dh), BF16), pltpu.VMEM((seq, dh), BF16)],
        compiler_params=_params(("parallel", "parallel"), blocks, 8 * _nbytes((tq, seq), F32)),
        name="fox_prompt_attention",
    )(qsrc, k, v, cq, ck, cast_w)


def _mem_prompt_body(q_ref, k_ref, v_ref, o_ref, *, scale):
    s = lax.dot_general(q_ref[...], k_ref[...].astype(BF16), NT_DIMS,
                        preferred_element_type=F32) * scale
    m = jnp.max(s, axis=1, keepdims=True)
    p = jnp.exp(s - m)
    p = p / jnp.sum(p, axis=1, keepdims=True)
    o_ref[...] = jnp.dot(p.astype(BF16), v_ref[...].astype(BF16),
                         preferred_element_type=F32).astype(o_ref.dtype)


def _mem_prompt(qsrc, q_col0, k, v, batch, seq, n_mem, heads, dh, tq):
    nq = seq // tq
    qoff = q_col0 // dh
    blocks = 2 * _nbytes((tq, dh), BF16) + 2 * _nbytes((n_mem, dh), F32)
    return pl.pallas_call(
        functools.partial(_mem_prompt_body, scale=dh ** -0.5),
        grid=(batch, heads, nq),
        in_specs=[pl.BlockSpec((tq, dh), lambda b, h, i: (b * nq + i, qoff + h)),
                  pl.BlockSpec((n_mem, dh), lambda b, h, i: (b, h)),
                  pl.BlockSpec((n_mem, dh), lambda b, h, i: (b, h))],
        out_specs=pl.BlockSpec((tq, dh), lambda b, h, i: (b * nq + i, h)),
        out_shape=jax.ShapeDtypeStruct((batch * seq, heads * dh), BF16),
        compiler_params=_params(("parallel", "parallel", "parallel"), blocks,
                                4 * _nbytes((tq, n_mem), F32) + _nbytes((tq, dh), F32)),
        name="memory_prompt_attention",
    )(qsrc, k, v)


HEAD_ROWS = 16


def _split_bf16(x):
    hi = x.astype(BF16)
    r1 = x - hi.astype(F32)
    mid = r1.astype(BF16)
    lo = (r1 - mid.astype(F32)).astype(BF16)
    return hi, mid, lo


def _fox_decode_body(pt_ref, q_ref, kn_ref, vn_ref, lfn_ref, *refs, heads, dh, pages_per_step, scale):
    npg = pages_per_step
    k_refs = refs[0:npg]
    v_refs = refs[npg:2 * npg]
    lf_refs = refs[2 * npg:3 * npg]
    o_ref = refs[3 * npg]
    spread_sc, m_sc, l_sc, acc_sc, carry_sc = refs[3 * npg + 1:]
    j = pl.program_id(1)
    flat = PAGE_SIZE * heads

    @pl.when(j == 0)
    def _():
        m_sc[...] = jnp.sum(q_ref[0].astype(F32) * kn_ref[0], axis=1, keepdims=True) * scale
        l_sc[...] = jnp.ones(l_sc.shape, F32)
        acc_sc[...] = vn_ref[0]
        carry_sc[...] = lfn_ref[0]
        key_of = lax.broadcasted_iota(jnp.int32, (PAGE_SIZE, flat), 1) // heads
        spread_sc[...] = (key_of == lax.broadcasted_iota(jnp.int32, (PAGE_SIZE, flat), 0)).astype(BF16)

    r = lax.broadcasted_iota(jnp.int32, (PAGE_SIZE, PAGE_SIZE), 0)
    c = lax.broadcasted_iota(jnp.int32, (PAGE_SIZE, PAGE_SIZE), 1)
    later = (r > c).astype(F32)
    carry = carry_sc[...]
    pieces = []
    for t in range(npg):
        lf = lf_refs[t][0]
        pieces.extend(_split_bf16(carry + jnp.dot(lf, later, precision=HIGHEST,
                                                  preferred_element_type=F32)))
        carry = carry + jnp.sum(lf, axis=1, keepdims=True)
    carry_sc[...] = carry
    bias_flat = jnp.dot(jnp.concatenate(pieces, axis=0), spread_sc[...], preferred_element_type=F32)

    own_head = (lax.broadcasted_iota(jnp.int32, (heads, flat), 1) % heads
                == lax.broadcasted_iota(jnp.int32, (heads, flat), 0))
    q = q_ref[0]
    scores = []
    for t in range(npg):
        b0 = 3 * heads * t
        bias = (bias_flat[b0:b0 + heads] + bias_flat[b0 + heads:b0 + 2 * heads]
                + bias_flat[b0 + 2 * heads:b0 + 3 * heads])
        k2 = k_refs[t][0].reshape(flat, dh).astype(BF16)
        s = lax.dot_general(q, k2, NT_DIMS, preferred_element_type=F32) * scale + bias
        scores.append(jnp.where(own_head, s, -jnp.inf))
    m_prev = m_sc[...]
    m_new = m_prev
    for s in scores:
        m_new = jnp.maximum(m_new, jnp.max(s, axis=1, keepdims=True))
    alpha = jnp.exp(m_prev - m_new)
    l_new = alpha * l_sc[...]
    acc = alpha * acc_sc[...]
    for t in range(npg):
        p = jnp.exp(scores[t] - m_new)
        l_new = l_new + jnp.sum(p, axis=1, keepdims=True)
        v2 = v_refs[t][0].reshape(flat, dh).astype(BF16)
        acc = acc + jnp.dot(p.astype(BF16), v2, preferred_element_type=F32)
    l_sc[...] = l_new
    acc_sc[...] = acc
    m_sc[...] = m_new

    @pl.when(j == pl.num_programs(1) - 1)
    def _():
        o_ref[0] = (acc_sc[...] / l_sc[...]).astype(o_ref.dtype)


def _fox_decode(q, k_new, v_new, lf_new, pool_k, pool_v, pool_lf_t, page_table, pages_per_step):
    db, heads, dh = q.shape
    assert heads == HEAD_ROWS and pool_k.shape[1] == PAGE_SIZE
    n_pages = page_table.shape[1]
    npg = pages_per_step
    steps = n_pages // npg

    def page_map(t, rank):
        return lambda b, j, pt: (pt[b, n_pages - 1 - (j * npg + t)],) + (0,) * (rank - 1)

    row = lambda b, j, pt: (b, 0, 0)
    in_specs = [pl.BlockSpec((1, heads, dh), row), pl.BlockSpec((1, heads, dh), row),
                pl.BlockSpec((1, heads, dh), row), pl.BlockSpec((1, heads, 1), row)]
    in_specs += [pl.BlockSpec((1, PAGE_SIZE, heads, dh), page_map(t, 4)) for t in range(npg)]
    in_specs += [pl.BlockSpec((1, PAGE_SIZE, heads, dh), page_map(t, 4)) for t in range(npg)]
    in_specs += [pl.BlockSpec((1, heads, PAGE_SIZE), page_map(t, 3)) for t in range(npg)]
    page_bytes = _nbytes((PAGE_SIZE, heads, dh), F32)
    blocks = npg * (2 * page_bytes + _nbytes((heads, PAGE_SIZE), F32))
    return pl.pallas_call(
        functools.partial(_fox_decode_body, heads=heads, dh=dh, pages_per_step=npg, scale=dh ** -0.5),
        grid_spec=pltpu.PrefetchScalarGridSpec(
            num_scalar_prefetch=1,
            grid=(db, steps),
            in_specs=in_specs,
            out_specs=pl.BlockSpec((1, heads, dh), row),
            scratch_shapes=[pltpu.VMEM((PAGE_SIZE, PAGE_SIZE * heads), BF16),
                            pltpu.VMEM((heads, 1), F32), pltpu.VMEM((heads, 1), F32),
                            pltpu.VMEM((heads, dh), F32), pltpu.VMEM((heads, 1), F32)]),
        out_shape=jax.ShapeDtypeStruct((db, heads, dh), BF16),
        compiler_params=_params(("parallel", "arbitrary"), blocks, 4 * page_bytes),
        name="fox_decode_attention",
    )(page_table, q, k_new, v_new, lf_new, *([pool_k] * npg), *([pool_v] * npg), *([pool_lf_t] * npg))


def _mem_decode_body(q_ref, k_ref, v_ref, o_ref, *, heads, scale):
    n_mem, _, dh = k_ref.shape
    flat = n_mem * heads
    q = jnp.concatenate([q_ref[0], jnp.zeros((HEAD_ROWS - heads, dh), q_ref.dtype)], axis=0)
    k2 = k_ref[...].reshape(flat, dh).astype(BF16)
    v2 = v_ref[...].reshape(flat, dh).astype(BF16)
    own_head = (lax.broadcasted_iota(jnp.int32, (HEAD_ROWS, flat), 1) % heads
                == lax.broadcasted_iota(jnp.int32, (HEAD_ROWS, flat), 0) % heads)
    s = lax.dot_general(q, k2, NT_DIMS, preferred_element_type=F32) * scale
    s = jnp.where(own_head, s, -jnp.inf)
    m = jnp.max(s, axis=1, keepdims=True)
    p = jnp.exp(s - m)
    l = jnp.sum(p, axis=1, keepdims=True)
    o = jnp.dot(p.astype(BF16), v2, preferred_element_type=F32)
    o_ref[0] = (o[0:heads] / l[0:heads]).astype(o_ref.dtype)


def _mem_decode(q, k, v, layer):
    db, heads, dh = q.shape
    n_mem = k.shape[2]
    assert heads <= HEAD_ROWS
    cache_spec = pl.BlockSpec((None, None, n_mem, heads, dh), lambda b: (layer, b, 0, 0, 0))
    blocks = 2 * _nbytes((n_mem, 8, dh), F32) + 2 * _nbytes((8, dh), F32)
    return pl.pallas_call(
        functools.partial(_mem_decode_body, heads=heads, scale=dh ** -0.5),
        grid=(db,),
        in_specs=[pl.BlockSpec((1, heads, dh), lambda b: (b, 0, 0)), cache_spec, cache_spec],
        out_specs=pl.BlockSpec((1, heads, dh), lambda b: (b, 0, 0)),
        out_shape=jax.ShapeDtypeStruct((db, heads, dh), BF16),
        compiler_params=_params(("parallel",), blocks, 6 * _nbytes((n_mem * heads, dh), F32)),
        name="memory_decode_attention",
    )(q, k, v)


def _head_norm_gate(h, w_row, og):
    hn = h * lax.rsqrt(jnp.mean(h * h, axis=-1, keepdims=True) + EPS)
    return hn * w_row * og.astype(F32)


def _mlstm_chunk_body(q_ref, k_ref, v_ref, og_ref, w_ref, gcol_ref, bcol_ref, irow_ref, brow_ref, wsrc_ref,
                      y_ref, c_ref, n_ref, m_ref, wdst_ref, *, k_scale, chunk):
    h = pl.program_id(1)
    L = chunk
    wdst_ref[...] = wsrc_ref[...].astype(wdst_ref.dtype)
    c_ref[...] = jnp.zeros(c_ref.shape, F32)
    n_ref[...] = jnp.zeros(n_ref.shape, F32)
    m_prev = jnp.zeros((1, 1), F32)
    r = lax.broadcasted_iota(jnp.int32, (L, L), 0)
    c = lax.broadcasted_iota(jnp.int32, (L, L), 1)
    causal = c <= r
    for ci in range(q_ref.shape[0] // L):
        rows = slice(ci * L, (ci + 1) * L)
        q = q_ref[rows, :]
        kf = k_ref[rows, :].astype(F32) * k_scale
        k = kf.astype(BF16)
        v = v_ref[rows, :]
        c_state = c_ref[0, 0]
        n_state = n_ref[0, 0]
        b_col = _lane_column(bcol_ref[0, rows, :], MLF_LANE0 + h)
        i_col = _lane_column(gcol_ref[0, rows, :], MIG_LANE0 + h)
        b_row = brow_ref[0, 0, :, rows]
        i_row = irow_ref[0, 0, :, rows]

        d = jnp.where(causal, b_col - b_row + i_row, -jnp.inf)
        inter = b_col + m_prev
        m_t = jnp.maximum(inter, jnp.max(d, axis=1, keepdims=True))
        w_intra = jnp.exp(d - m_t)
        w_inter = jnp.exp(inter - m_t)
        a = w_intra * lax.dot_general(q, k, NT_DIMS, preferred_element_type=F32)
        num = (jnp.dot(a.astype(BF16), v, preferred_element_type=F32)
               + w_inter * lax.dot_general(q, c_state.astype(BF16), NT_DIMS, preferred_element_type=F32))
        den = (jnp.sum(a, axis=1, keepdims=True)
               + w_inter * jnp.sum(q.astype(F32) * n_state, axis=1, keepdims=True))
        hcur = num / jnp.maximum(jnp.abs(den), jnp.exp(-m_t))
        y_ref[rows, :] = _head_norm_gate(hcur, w_ref[...], og_ref[rows, :]).astype(y_ref.dtype)

        m_new = m_t[L - 1:L, :]
        b_last = b_col[L - 1:L, :]
        g_inter = jnp.exp(b_last + m_prev - m_new)
        g_intra = jnp.exp(b_last - b_col + i_col - m_new)
        kg = g_intra * kf
        c_ref[0, 0] = g_inter * c_state + lax.dot_general(
            v, kg.astype(BF16), (((0,), (0,)), ((), ())), preferred_element_type=F32)
        n_ref[0, 0] = g_inter * n_state + jnp.sum(kg, axis=0, keepdims=True)
        m_prev = m_new
    m_ref[0, 0] = jnp.broadcast_to(m_prev, (1, LANES))


def _mlstm_prompt(p1, q_col0, k_col0, v_col0, p2, og_col0, norm_w, g, b_loc, i_row, b_row,
                  batch, seq, heads, dk, dv, k_scale, cast_w, cast_layer):
    L = M_CHUNK
    qo, ko, vo, oo = q_col0 // dk, k_col0 // dk, v_col0 // dv, og_col0 // dv
    w_in_spec, w_out_spec, w_shape, w_bytes = _ride_cast(
        cast_w, cast_layer, batch * heads, lambda b, h: b * heads + h)
    blocks = (2 * _nbytes((seq, dk), BF16) + 3 * _nbytes((seq, dv), BF16) + 2 * _nbytes((seq, LANES), F32)
              + _nbytes((dv, dk), F32) + w_bytes)
    return pl.pallas_call(
        functools.partial(_mlstm_chunk_body, k_scale=k_scale, chunk=L),
        grid=(batch, heads),
        in_specs=[pl.BlockSpec((seq, dk), lambda b, h: (b, qo + h)),
                  pl.BlockSpec((seq, dk), lambda b, h: (b, ko + h)),
                  pl.BlockSpec((seq, dv), lambda b, h: (b, vo + h)),
                  pl.BlockSpec((seq, dv), lambda b, h: (b, oo + h)),
                  pl.BlockSpec((1, dv), lambda b, h: (0, h)),
                  pl.BlockSpec((1, seq, LANES), lambda b, h: (b, 0, 0)),
                  pl.BlockSpec((1, seq, LANES), lambda b, h: (b, 0, 0)),
                  pl.BlockSpec((1, 1, 1, seq), lambda b, h: (b, h, 0, 0)),
                  pl.BlockSpec((1, 1, 1, seq), lambda b, h: (b, h, 0, 0)),
                  w_in_spec],
        out_specs=[pl.BlockSpec((seq, dv), lambda b, h: (b, h)),
                   pl.BlockSpec((1, 1, dv, dk), lambda b, h: (b, h, 0, 0)),
                   pl.BlockSpec((1, 1, 1, dk), lambda b, h: (b, h, 0, 0)),
                   pl.BlockSpec((1, 1, 1, LANES), lambda b, h: (b, h, 0, 0)),
                   w_out_spec],
        out_shape=[jax.ShapeDtypeStruct((batch * seq, heads * dv), BF16),
                   jax.ShapeDtypeStruct((batch, heads, dv, dk), F32),
                   jax.ShapeDtypeStruct((batch, heads, 1, dk), F32),
                   jax.ShapeDtypeStruct((batch, heads, 1, LANES), F32),
                   w_shape],
        compiler_params=_params(("parallel", "parallel"), blocks,
                                8 * _nbytes((L, L), F32) + 4 * _nbytes((L, dv), F32)
                                + 3 * _nbytes((dv, dk), F32)),
        name="mlstm_chunkwise",
    )(p1, p1, p1, p2, norm_w, g, b_loc, i_row, b_row, cast_w)


def _mlstm_step_body(q_ref, k_ref, v_ref, og_ref, w_ref, g_ref, c_ref, n_ref, m_ref,
                     y_ref, co_ref, no_ref, mo_ref, *, heads, dk, dv, k_scale):
    g = g_ref[0]
    eye = (lax.broadcasted_iota(jnp.int32, (dv, dv), 0)
           == lax.broadcasted_iota(jnp.int32, (dv, dv), 1)).astype(BF16)
    lane = lax.broadcasted_iota(jnp.int32, (1, LANES), 1)
    m_out = jnp.zeros((1, LANES), F32)
    for h in range(heads):
        q = q_ref[0, :, h * dk:(h + 1) * dk].astype(F32)
        kf = k_ref[0, :, h * dk:(h + 1) * dk].astype(F32) * k_scale
        v = v_ref[0, :, h * dv:(h + 1) * dv].astype(F32)
        ig = g[:, MIG_LANE0 + h:MIG_LANE0 + h + 1]
        lf = g[:, MLF_LANE0 + h:MLF_LANE0 + h + 1]
        m_prev = m_ref[0][:, h:h + 1]
        c_state = c_ref[0, h]
        n_state = n_ref[0, h]
        inter = lf + m_prev
        m_t = jnp.maximum(inter, ig)
        w_intra = jnp.exp(ig - m_t)
        w_inter = jnp.exp(inter - m_t)
        a = w_intra * jnp.sum(q * kf, axis=1, keepdims=True)
        q_rows = jnp.broadcast_to(q, (HEAD_ROWS, dk)).astype(BF16)
        cq = lax.dot_general(q_rows, c_state.astype(BF16), NT_DIMS, preferred_element_type=F32)[0:1]
        num = a * v + w_inter * cq
        den = a + w_inter * jnp.sum(q * n_state, axis=1, keepdims=True)
        hcur = num / jnp.maximum(jnp.abs(den), jnp.exp(-m_t))
        y_ref[0, :, h * dv:(h + 1) * dv] = _head_norm_gate(
            hcur, w_ref[:, h * dv:(h + 1) * dv], og_ref[0, :, h * dv:(h + 1) * dv]).astype(y_ref.dtype)
        v_rows = jnp.broadcast_to(v, (HEAD_ROWS, dv)).astype(BF16)
        v_col = lax.dot_general(eye, v_rows, NT_DIMS, preferred_element_type=F32)[:, 0:1]
        kg = w_intra * kf
        co_ref[0, h] = w_inter * c_state + v_col * kg
        no_ref[0, h] = w_inter * n_state + kg
        m_out = jnp.where(lane == h, m_t, m_out)
    mo_ref[0] = m_out


def _mlstm_step(q, k, v, og, norm_w, g, c0, n0, m0, heads, dk, dv, k_scale):
    db = q.shape[0]
    row = lambda b: (b, 0, 0)
    st = lambda b: (b, 0, 0, 0)
    blocks = 2 * _nbytes((heads, dv, dk), F32) + 4 * _nbytes((1, heads * dv), F32)
    return pl.pallas_call(
        functools.partial(_mlstm_step_body, heads=heads, dk=dk, dv=dv, k_scale=k_scale),
        grid=(db,),
        in_specs=[pl.BlockSpec((1, 1, heads * dk), row), pl.BlockSpec((1, 1, heads * dk), row),
                  pl.BlockSpec((1, 1, heads * dv), row), pl.BlockSpec((1, 1, heads * dv), row),
                  pl.BlockSpec((1, heads * dv), lambda b: (0, 0)),
                  pl.BlockSpec((1, 1, LANES), row),
                  pl.BlockSpec((1, heads, dv, dk), st), pl.BlockSpec((1, heads, 1, dk), st),
                  pl.BlockSpec((1, 1, heads), row)],
        out_specs=[pl.BlockSpec((1, 1, heads * dv), row),
                   pl.BlockSpec((1, heads, dv, dk), st), pl.BlockSpec((1, heads, 1, dk), st),
                   pl.BlockSpec((1, 1, LANES), row)],
        out_shape=[jax.ShapeDtypeStruct((db, 1, heads * dv), BF16),
                   jax.ShapeDtypeStruct((db, heads, dv, dk), F32),
                   jax.ShapeDtypeStruct((db, heads, 1, dk), F32),
                   jax.ShapeDtypeStruct((db, 1, LANES), F32)],
        compiler_params=_params(("parallel",), blocks, 6 * _nbytes((dv, dk), F32)),
        name="mlstm_step",
    )(q, k, v, og, norm_w, g, c0, n0, m0)


def kernel(x_prompt, x_sample, mem_prompt, cache_fox_k, cache_fox_v, cache_fox_logf, state_mlstm_C, state_mlstm_n, state_mlstm_m, cache_mem_k, cache_mem_v, page_table, attn_norm_w, w_in, fox_f_bias, m_i_bias, m_f_bias, m_norm_w, mem_norm_w, w_mem_kv, w_br_fox, w_br_m, w_br_mem, w_out, ffn_norm_w, w_up, w_down, final_norm_w):
    batch, seq, d_model = x_prompt.shape
    dec_batch, dec_seq, _ = x_sample.shape
    depth = w_in.shape[0]
    assert depth == 1 and dec_seq == 1
    fox_heads, fox_dh = cache_fox_k.shape[3], cache_fox_k.shape[4]
    m_heads, m_dv, m_dk = state_mlstm_C.shape[2:]
    n_mem, mem_heads, mem_dh = cache_mem_k.shape[2:]
    fox_w, mqk_w, mv_w, mem_w = fox_heads * fox_dh, m_heads * m_dk, m_heads * m_dv, mem_heads * mem_dh
    assert fox_heads == MIG_LANE0 and m_heads == MLF_LANE0 - MIG_LANE0
    tp, ts = batch * seq, dec_batch * dec_seq
    k_scale = m_dk ** -0.5

    sizes = (fox_w, fox_w, fox_w, fox_heads, mqk_w, mqk_w, mv_w, m_heads, m_heads, mv_w, mem_w,
             3 * d_model)
    offs = [0]
    for s in sizes:
        offs.append(offs[-1] + s)

    l = 0
    w_in_t = jnp.swapaxes(w_in, 1, 2)
    wa_cols, wb_cols, wc_cols = 3 * fox_w, 2 * mqk_w + mv_w, mv_w + mem_w + 3 * d_model
    cast_tk = min(d_model, 2048)
    w_a = _window_cast(w_in_t, l, offs[0], wa_cols, 512, cast_tk)
    w_b = _window_cast(w_in_t, l, offs[4], wb_cols, 512, cast_tk)
    w_c = _window_cast(w_in_t, l, offs[9], wc_cols, 512, cast_tk)
    PB_MQ, PB_MK, PB_MV = 0, mqk_w, 2 * mqk_w
    PC_OG, PC_QQ, PC_GATES = 0, mv_w, mv_w + mem_w
    n_small = fox_heads + 2 * m_heads
    w_sm = _gate_weights(w_in_t, l, offs[3], offs[7], fox_heads, 2 * m_heads)
    bias_sm = jnp.concatenate([fox_f_bias[l], m_i_bias[l], m_f_bias[l],
                               jnp.zeros((LANES - n_small,), F32)]).reshape(1, LANES).astype(F32)
    d_ff = w_up.shape[2]
    m_norm = m_norm_w[l].reshape(1, mv_w).astype(F32)

    def wide_tn(*col_counts):
        return 1024 if all(c % 1024 == 0 for c in col_counts) else 512

    def project(x2d, tm, ride_b=(), ride_c=()):
        h = _rmsnorm(x2d, attn_norm_w[l], BF16, min(tm, 256))
        fq = _matmul(h, w_a, n_cols=fox_w, col_off=0, out_dtype=BF16, tm=tm, tn=wide_tn(fox_w))
        fk = _matmul(h, w_a, n_cols=fox_w, col_off=fox_w, out_dtype=F32, tm=tm, tn=wide_tn(fox_w))
        fv = _matmul(h, w_a, n_cols=fox_w, col_off=2 * fox_w, out_dtype=F32, tm=tm, tn=wide_tn(fox_w))
        pb = _matmul(h, w_b, n_cols=wb_cols, out_dtype=BF16, tm=tm, tn=wide_tn(wb_cols), ride=ride_b)
        pc = _matmul(h, w_c, n_cols=wc_cols, out_dtype=BF16, tm=tm, tn=wide_tn(wc_cols, PC_QQ, PC_GATES),
                     epilogue="sigmoid", plain_cols=(PC_QQ, PC_GATES), ride=ride_c)
        sp = _matmul(h, w_sm, n_cols=LANES, out_dtype=F32, tm=tm, tn=LANES)
        return fq, fk, fv, pb, pc, sp

    def mix_and_mlp(x2d, y_fox, y_m, y_mem, pc, tm):
        merged = _merge(y_fox, y_m, y_mem, w_brf, w_brm, w_brq, pc, PC_GATES, tm, 256)
        x1 = _matmul(merged, w_out_b, n_cols=d_model, out_dtype=F32, tm=tm, tn=512, resid=x2d)
        h2 = _rmsnorm(x1, ffn_norm_w[l], BF16, min(tm, 256))
        up = _matmul(h2, w_up_b, n_cols=d_ff, out_dtype=BF16, tm=tm, tn=wide_tn(d_ff), epilogue="relu2")
        x2 = _matmul(up, w_down_b, n_cols=d_model, out_dtype=F32, tm=min(tm, 512), tn=256, resid=x1)
        return _rmsnorm(x2, final_norm_w, F32, min(tm, 256))

    xp = x_prompt.reshape(tp, d_model)
    fq, fk, fv, (pb, w_memkv_b), (pc, w_brf, w_brm, w_brq, w_out_b), sp = project(
        xp, 1024, ride_b=((w_mem_kv, l),),
        ride_c=((w_br_fox, l), (w_br_m, l), (w_br_mem, l), (w_out, l)))
    g, c_glob, b_loc = _gates(sp.reshape(batch, seq, LANES), bias_sm, M_CHUNK)
    ck = c_glob[:, :, FOX_LANE0:FOX_LANE0 + fox_heads].transpose(0, 2, 1).reshape(batch, fox_heads, 1, seq)
    i_row = g[:, :, MIG_LANE0:MIG_LANE0 + m_heads].transpose(0, 2, 1).reshape(batch, m_heads, 1, seq)
    b_row = b_loc[:, :, MLF_LANE0:MLF_LANE0 + m_heads].transpose(0, 2, 1).reshape(batch, m_heads, 1, seq)

    y_fox, w_down_b = _fox_prompt(fq, 0, fk, fv, c_glob, ck, batch, seq, fox_heads, fox_dh, 256, w_down, l)
    y_m, p_c, p_n, p_m, w_up_b = _mlstm_prompt(pb, PB_MQ, PB_MK, PB_MV, pc, PC_OG, m_norm, g, b_loc, i_row,
                                               b_row, batch, seq, m_heads, m_dk, m_dv, k_scale, w_up, l)
    mem_h = _rmsnorm(mem_prompt.reshape(batch * n_mem, d_model), mem_norm_w[l], BF16, 256)
    mem_k = _matmul(mem_h, w_memkv_b, n_cols=mem_w, col_off=0, out_dtype=F32, tm=batch * n_mem, tn=512)
    mem_v = _matmul(mem_h, w_memkv_b, n_cols=mem_w, col_off=mem_w, out_dtype=F32, tm=batch * n_mem, tn=512)
    y_mem = _mem_prompt(pc, PC_QQ, mem_k, mem_v, batch, seq, n_mem, mem_heads, mem_dh, min(seq, 2048))
    y_prompt = mix_and_mlp(xp, y_fox, y_m, y_mem, pc, 1024).reshape(batch, seq, d_model)

    xs = x_sample.reshape(ts, d_model)
    sfq, sfk, sfv, spb, spc, ssp = project(xs, ts)
    sg = _gates_single(ssp, bias_sm)
    s_flf = sg[:, FOX_LANE0:FOX_LANE0 + fox_heads]
    per_head = lambda a: a.reshape(ts, fox_heads, fox_dh)
    sy_fox = _fox_decode(
        per_head(sfq), per_head(sfk), per_head(sfv), s_flf.reshape(ts, fox_heads, 1),
        cache_fox_k[l], cache_fox_v[l], cache_fox_logf[l].transpose(0, 2, 1), page_table, 8).reshape(ts, fox_w)
    row3 = lambda a: a.reshape(ts, 1, a.shape[-1])
    sy_m, s_c, s_n, s_m = _mlstm_step(
        row3(spb[:, PB_MQ:PB_MQ + mqk_w]), row3(spb[:, PB_MK:PB_MK + mqk_w]), row3(spb[:, PB_MV:PB_MV + mv_w]),
        row3(spc[:, PC_OG:PC_OG + mv_w]), m_norm, row3(sg),
        state_mlstm_C[l], state_mlstm_n[l].reshape(ts, m_heads, 1, m_dk),
        state_mlstm_m[l].reshape(ts, 1, m_heads), m_heads, m_dk, m_dv, k_scale)
    sy_mem = _mem_decode(spc[:, PC_QQ:PC_QQ + mem_w].reshape(ts, mem_heads, mem_dh),
                         cache_mem_k, cache_mem_v, l).reshape(ts, mem_w)
    y_sample = mix_and_mlp(xs, sy_fox, sy_m.reshape(ts, mv_w), sy_mem, spc, ts).reshape(dec_batch, dec_seq, d_model)

    lead = lambda a, shape: a.reshape((1,) + shape)
    return (
        y_prompt, y_sample,
        lead(fk, (batch, seq, fox_heads, fox_dh)), lead(fv, (batch, seq, fox_heads, fox_dh)),
        lead(g[:, :, FOX_LANE0:FOX_LANE0 + fox_heads], (batch, seq, fox_heads)),
        lead(p_c, (batch, m_heads, m_dv, m_dk)), lead(p_n, (batch, m_heads, m_dk)),
        lead(p_m[:, :, 0, 0], (batch, m_heads)),
        lead(mem_k, (batch, n_mem, mem_heads, mem_dh)), lead(mem_v, (batch, n_mem, mem_heads, mem_dh)),
        lead(sfk, (dec_batch, dec_seq, fox_heads, fox_dh)), lead(sfv, (dec_batch, dec_seq, fox_heads, fox_dh)),
        lead(s_flf, (dec_batch, dec_seq, fox_heads)),
        lead(s_c, (dec_batch, m_heads, m_dv, m_dk)), lead(s_n, (dec_batch, m_heads, m_dk)),
        lead(s_m[:, 0, :m_heads], (dec_batch, m_heads)),
    )
```
